```python
import math
import jax
import jax.numpy as jnp
from jax import lax
import numpy as np

D_MODEL = 1024
BATCH = 16
SEQ = 2048
DEPTH = 1

CHUNK = 64
Q_BLOCK = 128
DA_HEADS = 8
DA_HEAD_DIM = D_MODEL // (2 * DA_HEADS)
DA_WIDTH = DA_HEADS * 2 * DA_HEAD_DIM
ML_HEADS = 8
ML_WIDTH = D_MODEL
ML_HEAD_DIM = ML_WIDTH // ML_HEADS
CONV_K = 4
N_BUCKETS = 32
MAX_DISTANCE = 128
NORM_EPS = 1e-6
SUBLN_EPS = 1e-5
HEAD_LN_EPS = 1e-5
NEG_INF = -1e30
IN_SIZES = (DA_WIDTH, DA_WIDTH, DA_WIDTH, DA_WIDTH,
            ML_WIDTH, ML_WIDTH, ML_WIDTH, ML_WIDTH, ML_WIDTH,
            ML_HEADS, ML_HEADS, D_MODEL, D_MODEL)
D_IN = 4 * DA_WIDTH + 5 * ML_WIDTH + 2 * ML_HEADS + 2 * D_MODEL

kernel_name = "hybrid_diffattn_mlstm_gated_block"


def _rmsnorm(x, w, eps=NORM_EPS):
    xf = x.astype(jnp.float32)
    y = xf * lax.rsqrt(jnp.mean(xf * xf, axis=-1, keepdims=True) + eps)
    return (y * w.astype(jnp.float32)).astype(x.dtype)


def _split_columns(w):
    offsets = [int(o) for o in np.cumsum(IN_SIZES)[:-1]]
    return jnp.split(w, offsets, axis=-1)


def _rel_bucket(rel):
    nb = N_BUCKETS // 2
    max_exact = nb // 2
    bucket = jnp.where(rel > 0, nb, 0)
    n = jnp.abs(rel)
    nf = jnp.maximum(n, 1).astype(jnp.float32)
    large = max_exact + (jnp.log(nf / max_exact) / math.log(MAX_DISTANCE / max_exact)
                         * (nb - max_exact)).astype(jnp.int32)
    large = jnp.minimum(large, nb - 1)
    return bucket + jnp.where(n < max_exact, n, large)


def _diff_attention(q, k, v, lam_vecs, subln_w, rel_bias, lambda_init):
    b, s, _ = q.shape
    out_dtype = q.dtype
    q = (q.reshape(b, s, DA_HEADS, 2, DA_HEAD_DIM) * DA_HEAD_DIM ** -0.5).transpose(3, 0, 2, 1, 4)
    k = k.reshape(b, s, DA_HEADS, 2, DA_HEAD_DIM).transpose(3, 0, 2, 1, 4)
    v = v.reshape(b, s, DA_HEADS, 2 * DA_HEAD_DIM).transpose(0, 2, 1, 3).astype(jnp.float32)
    lf = lam_vecs.astype(jnp.float32)
    lam = jnp.exp(jnp.sum(lf[0] * lf[1])) - jnp.exp(jnp.sum(lf[2] * lf[3])) + lambda_init
    kpos = jnp.arange(s)

    def block(i):
        q0 = i * Q_BLOCK
        qb = lax.dynamic_slice_in_dim(q, q0, Q_BLOCK, axis=3)
        qpos = q0 + jnp.arange(Q_BLOCK)
        bias = rel_bias[_rel_bucket(kpos[None, :] - qpos[:, None])]
        bias = bias.astype(jnp.float32).transpose(2, 0, 1)
        mask = (kpos[None, :] // CHUNK) <= (qpos[:, None] // CHUNK)
        logits = jnp.einsum("pbhqd,pbhkd->pbhqk", qb, k).astype(jnp.float32) + bias
        probs = jax.nn.softmax(jnp.where(mask, logits, NEG_INF), axis=-1)
        attn = probs[0] - lam * probs[1]
        return jnp.einsum("bhqk,bhkv->bhqv", attn, v)

    out = lax.map(block, jnp.arange(s // Q_BLOCK))
    out = out.transpose(1, 2, 0, 3, 4).reshape(b, DA_HEADS, s, 2 * DA_HEAD_DIM)
    out = _rmsnorm(out, subln_w, SUBLN_EPS) * (1.0 - lambda_init)
    return out.transpose(0, 2, 1, 3).reshape(b, s, DA_WIDTH).astype(out_dtype)


def _causal_conv(u, w, bias):
    s = u.shape[1]
    up = jnp.pad(u, ((0, 0), (CONV_K - 1, 0), (0, 0)))
    out = bias
    for j in range(CONV_K):
        out = out + w[j] * up[:, j:j + s, :]
    return out


def _mlstm_chunk(carry, inp):
    c_state, n_state, m_state = carry
    q, k, v, ig, lf = inp
    q = q.astype(jnp.float32)
    k = k.astype(jnp.float32)
    v = v.astype(jnp.float32)
    b = jnp.cumsum(lf, axis=-1)
    causal = jnp.arange(CHUNK)[:, None] >= jnp.arange(CHUNK)[None, :]
    d_log = jnp.where(causal, b[..., :, None] - b[..., None, :] + ig[..., None, :], NEG_INF)
    inter = b + m_state[..., None]
    m_t = jnp.maximum(inter, jnp.max(d_log, axis=-1))
    w_intra = jnp.exp(d_log - m_t[..., None])
    w_inter = jnp.exp(inter - m_t)
    scores = jnp.einsum("bhtd,bhsd->bhts", q, k) * w_intra
    num = (jnp.einsum("bhts,bhsv->bhtv", scores, v)
           + w_inter[..., None] * jnp.einsum("bhtd,bhdv->bhtv", q, c_state))
    den = jnp.sum(scores, axis=-1) + w_inter * jnp.einsum("bhtd,bhd->bht", q, n_state)
    h = num / jnp.maximum(jnp.abs(den), jnp.exp(-m_t))[..., None]
    b_last = b[..., -1]
    w_log = b_last[..., None] - b + ig
    m_new = jnp.maximum(b_last + m_state, jnp.max(w_log, axis=-1))
    w_s = jnp.exp(w_log - m_new[..., None])
    decay = jnp.exp(b_last + m_state - m_new)
    c_new = decay[..., None, None] * c_state + jnp.einsum("bhs,bhsd,bhsv->bhdv", w_s, k, v)
    n_new = decay[..., None] * n_state + jnp.einsum("bhs,bhsd->bhd", w_s, k)
    return (c_new, n_new, m_new), h


def _mlstm(q, k, v, ig, fg, o, conv_w, conv_b, b_if, mh_w):
    b, s, _ = q.shape
    nc = s // CHUNK
    out_dtype = v.dtype
    q = jax.nn.silu(_causal_conv(q, conv_w[0], conv_b[0]))
    k = jax.nn.silu(_causal_conv(k, conv_w[1], conv_b[1])) * ML_HEAD_DIM ** -0.5

    def heads_to_chunks(u):
        return u.reshape(b, nc, CHUNK, ML_HEADS, ML_HEAD_DIM).transpose(1, 0, 3, 2, 4)

    def gates_to_chunks(g):
        return g.reshape(b, nc, CHUNK, ML_HEADS).transpose(1, 0, 3, 2)

    ig_c = gates_to_chunks((ig + b_if[0]).astype(jnp.float32))
    lf_c = gates_to_chunks(jax.nn.log_sigmoid((fg + b_if[1]).astype(jnp.float32)))
    init = (jnp.zeros((b, ML_HEADS, ML_HEAD_DIM, ML_HEAD_DIM), jnp.float32),
            jnp.zeros((b, ML_HEADS, ML_HEAD_DIM), jnp.float32),
            jnp.zeros((b, ML_HEADS), jnp.float32))
    _, h = lax.scan(_mlstm_chunk, init,
                    (heads_to_chunks(q), heads_to_chunks(k), heads_to_chunks(v), ig_c, lf_c))
    h = h.transpose(1, 0, 3, 2, 4).reshape(b, s, ML_HEADS, ML_HEAD_DIM)
    mu = jnp.mean(h, axis=-1, keepdims=True)
    var = jnp.mean(jnp.square(h - mu), axis=-1, keepdims=True)
    hn = (h - mu) * lax.rsqrt(var + HEAD_LN_EPS) * mh_w.astype(jnp.float32).reshape(ML_HEADS, ML_HEAD_DIM)
    return (jax.nn.sigmoid(o.astype(jnp.float32)) * hn.reshape(b, s, ML_WIDTH)).astype(out_dtype)


def _normal(key, shape, scale):
    return scale * jax.random.normal(key, shape, jnp.float32)


def setup_inputs(seed: int = 0) -> dict:
    key = jax.random.key(seed)
    ks = jax.random.split(key, 18)
    b_if = jnp.stack([
        _normal(ks[8], (DEPTH, ML_HEADS), 0.1),
        jnp.linspace(3.0, 6.0, ML_HEADS, dtype=jnp.float32)[None, :] + _normal(ks[9], (DEPTH, ML_HEADS), 0.1),
    ], axis=1)
    return {
        "x": _normal(ks[0], (BATCH, SEQ, D_MODEL), 1.0),
        "norm_w": 1.0 + _normal(ks[1], (DEPTH, D_MODEL), 0.02),
        "w_in": _normal(ks[2], (DEPTH, D_MODEL, D_IN), D_MODEL ** -0.5),
        "lam": _normal(ks[3], (DEPTH, 4, DA_HEAD_DIM), 0.1),
        "subln_w": 1.0 + _normal(ks[4], (DEPTH, 2 * DA_HEAD_DIM), 0.02),
        "rel_bias": _normal(ks[5], (N_BUCKETS, DA_HEADS), 0.1),
        "conv_w": _normal(ks[6], (DEPTH, 2, CONV_K, ML_WIDTH), CONV_K ** -0.5),
        "conv_b": _normal(ks[7], (DEPTH, 2, ML_WIDTH), 0.02),
        "b_if": b_if,
        "mh_w": 1.0 + _normal(ks[10], (DEPTH, ML_WIDTH), 0.02),
        "b_gate": _normal(ks[11], (DEPTH, 2, D_MODEL), 0.02),
        "w_pa": _normal(ks[12], (DEPTH, DA_WIDTH, D_MODEL), DA_WIDTH ** -0.5),
        "w_pm": _normal(ks[13], (DEPTH, ML_WIDTH, D_MODEL), ML_WIDTH ** -0.5),
        "w_out": _normal(ks[14], (DEPTH, D_MODEL, D_MODEL), D_MODEL ** -0.5),
        "norm_final": 1.0 + _normal(ks[15], (D_MODEL,), 0.02),
    }


def reference(x, norm_w, w_in, lam, subln_w, rel_bias, conv_w, conv_b, b_if, mh_w,
              b_gate, w_pa, w_pm, w_out, norm_final):
    h = x
    for layer in range(DEPTH):
        lambda_init = 0.8 - 0.6 * math.exp(-0.3 * layer)
        xn = _rmsnorm(h, norm_w[layer])
        (da_q, da_k, da_v, da_z, ml_q, ml_k, ml_v, ml_o, ml_z, ml_i, ml_f,
         gate_a, gate_m) = [xn @ wp for wp in _split_columns(w_in[layer])]
        y_a = _diff_attention(da_q, da_k, da_v, lam[layer], subln_w[layer], rel_bias,
                              lambda_init) * jax.nn.silu(da_z)
        y_m = _mlstm(ml_q, ml_k, ml_v, ml_i, ml_f, ml_o, conv_w[layer], conv_b[layer],
                     b_if[layer], mh_w[layer]) * jax.nn.silu(ml_z)
        g_a = jax.nn.sigmoid(gate_a + b_gate[layer, 0])
        g_m = jax.nn.sigmoid(gate_m + b_gate[layer, 1])
        merged = g_a * (y_a @ w_pa[layer]) + g_m * (y_m @ w_pm[layer])
        h = h + merged @ w_out[layer]
    return _rmsnorm(h, norm_final)
```

```python
import functools
import math

import numpy as np
import jax
import jax.numpy as jnp
from jax import lax
from jax.experimental import pallas as pl
from jax.experimental.pallas import tpu as pltpu

F32 = jnp.float32
BF16 = jnp.bfloat16

D_MODEL = 1024
N_HEADS = 8
HEAD_W = 128
DA_HEAD_DIM = 64
CHUNK = 64
CONV_K = 4
N_BUCKETS = 32
MAX_DISTANCE = 128
NORM_EPS = 1e-6
SUBLN_EPS = 1e-5
HEAD_LN_EPS = 1e-5
NEG_INF = -1e30

COL_DA_Q, COL_DA_K, COL_DA_V, COL_DA_Z = 0, 1, 2, 3
COL_ML_Q, COL_ML_K, COL_ML_V, COL_ML_O, COL_ML_Z = 4, 5, 6, 7, 8
COL_GATE_A, COL_GATE_M = 9, 10
N_COL_BLOCKS = 11

ATT_BLOCK = 256
ML_CHUNK = 128
VMEM_LIMIT = 56 * 1024 * 1024


def _sigmoid(x):
    return 1.0 / (1.0 + jnp.exp(-x))


def _dot(a, b):
    return jnp.dot(a, b, preferred_element_type=F32)


def _dot_nt(a, b):
    return lax.dot_general(a, b, (((1,), (1,)), ((), ())), preferred_element_type=F32)


def _dot_tn(a, b):
    return lax.dot_general(a, b, (((0,), (0,)), ((), ())), preferred_element_type=F32)


def _inproj_kernel(x_ref, nw_ref, w_ref, wg_ref, o_ref, g_ref, xn_ref):
    @pl.when(pl.program_id(1) == 0)
    def _():
        x = x_ref[...]
        ms = jnp.mean(x * x, axis=-1, keepdims=True)
        xn = (x * lax.rsqrt(ms + NORM_EPS) * nw_ref[...]).astype(BF16)
        xn_ref[...] = xn
        g_ref[...] = _dot(xn, wg_ref[...])

    o_ref[...] = _dot(xn_ref[...], w_ref[...]).astype(BF16)


def _inproj(x2, norm_w, w_big, w_gate, tm, tn):
    m = x2.shape[0]
    n = w_big.shape[1]
    return pl.pallas_call(
        _inproj_kernel,
        grid=(m // tm, n // tn),
        in_specs=[
            pl.BlockSpec((tm, D_MODEL), lambda i, j: (i, 0)),
            pl.BlockSpec((1, D_MODEL), lambda i, j: (0, 0)),
            pl.BlockSpec((D_MODEL, tn), lambda i, j: (0, j)),
            pl.BlockSpec((D_MODEL, 2 * HEAD_W), lambda i, j: (0, 0)),
        ],
        out_specs=[
            pl.BlockSpec((tm, tn), lambda i, j: (i, j)),
            pl.BlockSpec((tm, 2 * HEAD_W), lambda i, j: (i, 0)),
        ],
        out_shape=[
            jax.ShapeDtypeStruct((m, n), BF16),
            jax.ShapeDtypeStruct((m, 2 * HEAD_W), F32),
        ],
        scratch_shapes=[pltpu.VMEM((tm, D_MODEL), BF16)],
        compiler_params=pltpu.CompilerParams(
            dimension_semantics=("arbitrary", "arbitrary"), vmem_limit_bytes=VMEM_LIMIT),
        name="inproj",
    )(x2, norm_w, w_big, w_gate)


def _rel_bucket_np(rel):
    nb = N_BUCKETS // 2
    max_exact = nb // 2
    bucket = np.where(rel > 0, nb, 0)
    n = np.abs(rel)
    nf = np.maximum(n, 1).astype(np.float32)
    large = max_exact + (np.log(nf / np.float32(max_exact)) / np.float32(math.log(MAX_DISTANCE / max_exact))
                         * np.float32(nb - max_exact)).astype(np.int32)
    large = np.minimum(large, nb - 1)
    return bucket + np.where(n < max_exact, n, large)


def _bucket_tiles():
    dk = np.arange(ATT_BLOCK)[:, None]
    dq = np.arange(ATT_BLOCK)[None, :]
    diag = np.where(dk // CHUNK <= dq // CHUNK, _rel_bucket_np(dk - dq), -1)
    prev = _rel_bucket_np(dk - dq - ATT_BLOCK)
    return np.stack([diag, prev]).astype(np.int32)


FAR_BUCKET = int(_rel_bucket_np(np.array([-(ATT_BLOCK + 1)]))[0])


def _bias_kernel(rb_ref, bkt_ref, o_ref):
    h = pl.program_id(0)
    for t in range(2):
        bk = bkt_ref[t]
        acc = jnp.full((ATT_BLOCK, ATT_BLOCK), NEG_INF, F32)
        for n in range(N_BUCKETS):
            acc = jnp.where(bk == n, rb_ref[n, h], acc)
        o_ref[0, t] = acc


def _bias_tiles(rel_bias):
    return pl.pallas_call(
        _bias_kernel,
        grid=(N_HEADS,),
        in_specs=[
            pl.BlockSpec(memory_space=pltpu.SMEM),
            pl.BlockSpec((2, ATT_BLOCK, ATT_BLOCK), lambda h: (0, 0, 0)),
        ],
        out_specs=pl.BlockSpec((1, 2, ATT_BLOCK, ATT_BLOCK), lambda h: (h, 0, 0, 0)),
        out_shape=jax.ShapeDtypeStruct((N_HEADS, 2, ATT_BLOCK, ATT_BLOCK), F32),
        name="bias_tiles",
    )(rel_bias, jnp.asarray(_bucket_tiles()))


def _attn_kernel(rb_ref, lam_ref, subln_ref, bias_ref, q_ref, k_ref, v_ref, z_ref, o_ref,
                 vt_ref, acc_ref, *, seq, lambda_init):
    h = pl.program_id(1)
    tb = ATT_BLOCK
    lamv = lam_ref[...]
    lam = (jnp.exp(jnp.sum(lamv[0:1] * lamv[1:2], axis=-1, keepdims=True))
           - jnp.exp(jnp.sum(lamv[2:3] * lamv[3:4], axis=-1, keepdims=True)) + lambda_init)
    c_far = rb_ref[FAR_BUCKET, h]

    for i in range(seq // HEAD_W):
        sl = slice(i * HEAD_W, (i + 1) * HEAD_W)
        vt_ref[:, sl] = v_ref[0, sl, :].astype(F32).T.astype(BF16)

    lane = lax.broadcasted_iota(jnp.int32, (tb, HEAD_W), 1)

    def q_block(qi, _):
        q0 = pl.multiple_of(qi * tb, tb)
        q = (q_ref[0, pl.ds(q0, tb), :].astype(F32) * DA_HEAD_DIM ** -0.5).astype(BF16)
        q_halves = (jnp.where(lane < DA_HEAD_DIM, q, jnp.zeros_like(q)),
                    jnp.where(lane >= DA_HEAD_DIM, q, jnp.zeros_like(q)))
        acc_ref[...] = jnp.zeros_like(acc_ref)

        def kv_step(k0, bias, carry):
            kb = k_ref[0, pl.ds(k0, tb), :]
            vtb = vt_ref[:, pl.ds(k0, tb)]
            out = []
            for p in range(2):
                m_old, l_old = carry[p]
                s = _dot_nt(kb, q_halves[p]) + bias
                m_new = jnp.maximum(m_old, jnp.max(s, axis=0, keepdims=True))
                alpha = jnp.exp(m_old - m_new)
                pt = jnp.exp(s - m_new)
                l_new = alpha * l_old + jnp.sum(pt, axis=0, keepdims=True)
                acc_ref[p] = alpha * acc_ref[p] + _dot(vtb, pt.astype(BF16))
                out.append((m_new, l_new))
            return tuple(out)

        init = ((jnp.full((1, tb), NEG_INF, F32), jnp.zeros((1, tb), F32)),) * 2
        carry = lax.fori_loop(
            0, jnp.maximum(qi - 1, 0),
            lambda j, c: kv_step(pl.multiple_of(j * tb, tb), c_far, c), init)
        carry = lax.cond(
            qi >= 1,
            lambda c: kv_step(pl.multiple_of((qi - 1) * tb, tb), bias_ref[0, 1], c),
            lambda c: c, carry)
        (_, l1), (_, l2) = kv_step(q0, bias_ref[0, 0], carry)

        o = acc_ref[0] * (1.0 / l1) - lam * (acc_ref[1] * (1.0 / l2))
        ms = jnp.mean(o * o, axis=0, keepdims=True)
        on = (o * lax.rsqrt(ms + SUBLN_EPS)).T
        z = z_ref[0, pl.ds(q0, tb), :].astype(F32)
        y = on * subln_ref[...] * (1.0 - lambda_init) * (z * _sigmoid(z))
        o_ref[0, pl.ds(q0, tb), :] = y.astype(o_ref.dtype)
        return 0

    lax.fori_loop(0, seq // tb, q_block, 0)


def _attention(proj3, rel_bias, bias_tiles, lam, subln_w, lambda_init):
    b, s, _ = proj3.shape

    def col(block):
        return pl.BlockSpec((1, s, HEAD_W), lambda bi, hi: (bi, 0, block * N_HEADS + hi))

    return pl.pallas_call(
        functools.partial(_attn_kernel, seq=s, lambda_init=lambda_init),
        grid=(b, N_HEADS),
        in_specs=[
            pl.BlockSpec(memory_space=pltpu.SMEM),
            pl.BlockSpec((4, DA_HEAD_DIM), lambda bi, hi: (0, 0)),
            pl.BlockSpec((1, HEAD_W), lambda bi, hi: (0, 0)),
            pl.BlockSpec((1, 2, ATT_BLOCK, ATT_BLOCK), lambda bi, hi: (hi, 0, 0, 0)),
            col(COL_DA_Q), col(COL_DA_K), col(COL_DA_V), col(COL_DA_Z),
        ],
        out_specs=pl.BlockSpec((1, s, HEAD_W), lambda bi, hi: (bi, 0, hi)),
        out_shape=jax.ShapeDtypeStruct((b, s, D_MODEL), BF16),
        scratch_shapes=[
            pltpu.VMEM((HEAD_W, s), BF16),
            pltpu.VMEM((2, HEAD_W, ATT_BLOCK), F32),
        ],
        compiler_params=pltpu.CompilerParams(
            dimension_semantics=("arbitrary", "arbitrary"), vmem_limit_bytes=VMEM_LIMIT),
        name="diff_attention",
    )(rel_bias, lam, subln_w, bias_tiles, proj3, proj3, proj3, proj3)


_BCAST_PARTS = (3, 3, 2, 2, 2)


def _bcast_select_np():
    n_out = len(_BCAST_PARTS)
    e = np.zeros((N_HEADS, HEAD_W, n_out * HEAD_W), np.float32)
    g = 0
    for o, parts in enumerate(_BCAST_PARTS):
        for _ in range(parts):
            for h in range(N_HEADS):
                e[h, 8 * g + h, o * HEAD_W:(o + 1) * HEAD_W] = 1.0
            g += 1
    return e


def _split_parts(x, n):
    parts = []
    r = x
    for i in range(n):
        p = r.astype(BF16).astype(F32)
        parts.append(p)
        if i + 1 < n:
            r = r - p
    return parts


def _mlstm_kernel(q_ref, k_ref, v_ref, o_ref, z_ref, g_ref, cw_ref, cb_ref, bif_ref, mhw_ref,
                  ltri_ref, esel_ref, y_ref, qx_ref, kx_ref, qc_ref, kc_ref, c_ref, m_ref, *, tblk):
    L = ML_CHUNK

    @pl.when(pl.program_id(1) == 0)
    def _():
        qx_ref[0:8, :] = jnp.zeros((8, D_MODEL), F32)
        kx_ref[0:8, :] = jnp.zeros((8, D_MODEL), F32)
        c_ref[...] = jnp.zeros_like(c_ref)
        m_ref[...] = jnp.zeros_like(m_ref)

    qx_ref[8:, :] = q_ref[0].astype(F32)
    kx_ref[8:, :] = k_ref[0].astype(F32)
    accq = cb_ref[0:1, :]
    acck = cb_ref[1:2, :]
    for j in range(CONV_K):
        accq = accq + cw_ref[0, j:j + 1, :] * qx_ref[pl.ds(8 - (CONV_K - 1) + j, tblk), :]
        acck = acck + cw_ref[1, j:j + 1, :] * kx_ref[pl.ds(8 - (CONV_K - 1) + j, tblk), :]
    qc_ref[...] = accq * _sigmoid(accq)
    kc_ref[...] = (acck * _sigmoid(acck) * HEAD_W ** -0.5).astype(BF16)
    qx_ref[0:8, :] = qx_ref[tblk:tblk + 8, :]
    kx_ref[0:8, :] = kx_ref[tblk:tblk + 8, :]

    row = lax.broadcasted_iota(jnp.int32, (L, HEAD_W), 0)
    colv = lax.broadcasted_iota(jnp.int32, (L, HEAD_W), 1)
    causal = colv <= row
    group = colv // 8
    ones_blk = jnp.ones((L, HEAD_W), BF16)
    ltri = ltri_ref[...]

    def chunk(c, _):
        r0 = pl.multiple_of(c * L, L)
        rows = pl.ds(r0, L)
        m_state = m_ref[...]
        ig = g_ref[0, rows, 0:HEAD_W] + bif_ref[0:1, :]
        fg = g_ref[0, rows, HEAD_W:2 * HEAD_W] + bif_ref[1:2, :]
        lf = jnp.minimum(fg, 0.0) - jnp.log(1.0 + jnp.exp(-jnp.abs(fg)))
        bcum = sum(_dot(ltri, p.astype(BF16)) for p in _split_parts(lf, 3))
        wp = ig - bcum
        cmax = wp
        sh = 1
        while sh < L:
            cmax = jnp.where(row >= sh, jnp.maximum(cmax, pltpu.roll(cmax, sh, 0)), cmax)
            sh *= 2
        inter = bcum + m_state
        m_t = jnp.maximum(inter, bcum + cmax)
        u = bcum - m_t
        w_inter = jnp.exp(inter - m_t)
        b_last = bcum[L - 1:L, :]
        w_log = b_last + wp
        m_new = jnp.maximum(b_last + m_state, jnp.max(w_log, axis=0, keepdims=True))
        w_state = jnp.exp(w_log - m_new)
        decay = jnp.broadcast_to(jnp.exp(b_last + m_state - m_new), (L, HEAD_W))
        m_ref[...] = m_new

        parts = []
        for x, n in zip((u, m_t, w_inter, w_state, decay), _BCAST_PARTS):
            parts.extend(_split_parts(x, n))
        packed = jnp.zeros((L, HEAD_W), F32)
        for gi, p in enumerate(parts):
            packed = jnp.where(group == gi, p, packed)
        packed = packed.astype(BF16)
        wp_t = wp.T

        for h in range(N_HEADS):
            hs = slice(h * HEAD_W, (h + 1) * HEAD_W)
            bc = _dot(packed, esel_ref[h])
            u_b = bc[:, 0:HEAD_W]
            m_b = bc[:, HEAD_W:2 * HEAD_W]
            wi_b = bc[:, 2 * HEAD_W:3 * HEAD_W]
            ws_b = bc[:, 3 * HEAD_W:4 * HEAD_W]
            dc_b = bc[:, 4 * HEAD_W:5 * HEAD_W]
            w_intra = jnp.exp(jnp.where(causal, u_b + wp_t[h:h + 1, :], NEG_INF))
            qf = qc_ref[rows, hs]
            kb = kc_ref[rows, hs]
            vb = v_ref[0, rows, hs]
            scores = (_dot_nt(qf.astype(BF16), kb) * w_intra).astype(BF16)
            lhs = jnp.concatenate([scores, (qf * wi_b).astype(BF16)], axis=1)
            c_old = c_ref[h]
            rhs = jnp.concatenate(
                [jnp.concatenate([vb, ones_blk], axis=1), c_old.astype(BF16)], axis=0)
            tot = _dot(lhs, rhs)
            hh = tot[:, 0:HEAD_W] / jnp.maximum(jnp.abs(tot[:, HEAD_W:]), jnp.exp(-m_b))
            mu = jnp.mean(hh, axis=-1, keepdims=True)
            hc = hh - mu
            var = jnp.mean(hc * hc, axis=-1, keepdims=True)
            hn = hc * lax.rsqrt(var + HEAD_LN_EPS) * mhw_ref[:, hs]
            og = o_ref[0, rows, hs].astype(F32)
            zg = z_ref[0, rows, hs].astype(F32)
            y_ref[0, rows, hs] = ((_sigmoid(og) * hn) * (zg * _sigmoid(zg))).astype(y_ref.dtype)

            wv = jnp.concatenate([(ws_b * vb.astype(F32)).astype(BF16), ws_b.astype(BF16)], axis=1)
            upd = _dot_tn(kb, wv)
            c_ref[h] = jnp.concatenate([dc_b, dc_b], axis=1) * c_old + upd
        return 0

    lax.fori_loop(0, tblk // L, chunk, 0)


def _mlstm(proj3, gates3, conv_w, conv_b, bif, mh_w, tblk):
    b, s, _ = proj3.shape

    def col(block):
        return pl.BlockSpec((1, tblk, D_MODEL), lambda bi, ti: (bi, ti, block))

    def full(shape):
        return pl.BlockSpec(shape, lambda bi, ti: (0,) * len(shape))

    ltri = jnp.asarray(np.tril(np.ones((ML_CHUNK, ML_CHUNK), np.float32)), BF16)
    esel = jnp.asarray(_bcast_select_np(), BF16)
    return pl.pallas_call(
        functools.partial(_mlstm_kernel, tblk=tblk),
        grid=(b, s // tblk),
        in_specs=[
            col(COL_ML_Q), col(COL_ML_K), col(COL_ML_V), col(COL_ML_O), col(COL_ML_Z),
            pl.BlockSpec((1, tblk, 2 * HEAD_W), lambda bi, ti: (bi, ti, 0)),
            full((2, CONV_K, D_MODEL)), full((2, D_MODEL)), full((2, HEAD_W)), full((1, D_MODEL)),
            full((ML_CHUNK, ML_CHUNK)), full(esel.shape),
        ],
        out_specs=pl.BlockSpec((1, tblk, D_MODEL), lambda bi, ti: (bi, ti, 0)),
        out_shape=jax.ShapeDtypeStruct((b, s, D_MODEL), BF16),
        scratch_shapes=[
            pltpu.VMEM((tblk + 8, D_MODEL), F32),
            pltpu.VMEM((tblk + 8, D_MODEL), F32),
            pltpu.VMEM((tblk, D_MODEL), F32),
            pltpu.VMEM((tblk, D_MODEL), BF16),
            pltpu.VMEM((N_HEADS, HEAD_W, 2 * HEAD_W), F32),
            pltpu.VMEM((1, HEAD_W), F32),
        ],
        compiler_params=pltpu.CompilerParams(
            dimension_semantics=("arbitrary", "arbitrary"), vmem_limit_bytes=VMEM_LIMIT),
        name="mlstm",
    )(proj3, proj3, proj3, proj3, proj3, gates3, conv_w, conv_b, bif, mh_w, ltri, esel)


def _out_kernel(ya_ref, ym_ref, ga_ref, gm_ref, x_ref, wpa_ref, wpm_ref, wout_ref, bg_ref, nf_ref,
                o_ref, *, final_norm):
    g_a = _sigmoid(ga_ref[...].astype(F32) + bg_ref[0:1, :])
    g_m = _sigmoid(gm_ref[...].astype(F32) + bg_ref[1:2, :])
    merged = g_a * _dot(ya_ref[...], wpa_ref[...]) + g_m * _dot(ym_ref[...], wpm_ref[...])
    hres = x_ref[...] + _dot(merged.astype(BF16), wout_ref[...])
    if final_norm:
        ms = jnp.mean(hres * hres, axis=-1, keepdims=True)
        hres = hres * lax.rsqrt(ms + NORM_EPS) * nf_ref[...]
    o_ref[...] = hres


def _out_stage(y_a, y_m, proj, x2, w_pa, w_pm, w_out, b_gate, norm_final, tm, final_norm):
    m = x2.shape[0]

    def rows(block=0):
        return pl.BlockSpec((tm, D_MODEL), lambda i: (i, block))

    def full(shape):
        return pl.BlockSpec(shape, lambda i: (0,) * len(shape))

    return pl.pallas_call(
        functools.partial(_out_kernel, final_norm=final_norm),
        grid=(m // tm,),
        in_specs=[
            rows(), rows(), rows(COL_GATE_A), rows(COL_GATE_M), rows(),
            full((D_MODEL, D_MODEL)), full((D_MODEL, D_MODEL)), full((D_MODEL, D_MODEL)),
            full((2, D_MODEL)), full((1, D_MODEL)),
        ],
        out_specs=rows(),
        out_shape=jax.ShapeDtypeStruct((m, D_MODEL), F32),
        compiler_params=pltpu.CompilerParams(
            dimension_semantics=("arbitrary",), vmem_limit_bytes=VMEM_LIMIT),
        name="out_stage",
    )(y_a, y_m, proj, proj, x2, w_pa, w_pm, w_out, b_gate, norm_final)


def _regroup_in_weight(w):
    n_wide = 9 * D_MODEL
    w_big = jnp.concatenate([w[:, :n_wide], w[:, n_wide + 2 * N_HEADS:]], axis=1).astype(BF16)
    w_i = w[:, n_wide:n_wide + N_HEADS]
    w_f = w[:, n_wide + N_HEADS:n_wide + 2 * N_HEADS]
    reps = HEAD_W // N_HEADS
    w_gate = jnp.concatenate([jnp.tile(w_i, (1, reps)), jnp.tile(w_f, (1, reps))], axis=1).astype(BF16)
    return w_big, w_gate


def kernel(x, norm_w, w_in, lam, subln_w, rel_bias, conv_w, conv_b, b_if, mh_w, b_gate, w_pa, w_pm, w_out,
           norm_final):
    b, s, d = x.shape
    depth = norm_w.shape[0]
    m = b * s
    tm_in = min(1024, m)
    tm_out = min(512, m)
    tblk = min(512, s)
    bias_tiles = _bias_tiles(rel_bias)
    h2 = x.reshape(m, d)
    for layer in range(depth):
        lambda_init = 0.8 - 0.6 * math.exp(-0.3 * layer)
        w_big, w_gate = _regroup_in_weight(w_in[layer])
        proj, gates = _inproj(h2, norm_w[layer][None, :], w_big, w_gate, tm_in, D_MODEL)
        proj3 = proj.reshape(b, s, N_COL_BLOCKS * D_MODEL)
        y_a = _attention(proj3, rel_bias, bias_tiles, lam[layer], subln_w[layer][None, :], lambda_init)
        bif = jnp.tile(b_if[layer], (1, HEAD_W // N_HEADS))
        y_m = _mlstm(proj3, gates.reshape(b, s, 2 * HEAD_W), conv_w[layer], conv_b[layer], bif,
                     mh_w[layer][None, :], tblk)
        h2 = _out_stage(y_a.reshape(m, d), y_m.reshape(m, d), proj, h2,
                        w_pa[layer].astype(BF16), w_pm[layer].astype(BF16), w_out[layer].astype(BF16),
                        b_gate[layer], norm_final[None, :], tm_out, final_norm=(layer == depth - 1))
    return h2.reshape(b, s, d)
```

```python
import functools
import math

import numpy as np
import jax
import jax.numpy as jnp
from jax import lax
from jax.experimental import pallas as pl
from jax.experimental.pallas import tpu as pltpu

F32 = jnp.float32
BF16 = jnp.bfloat16

D_MODEL = 1024
N_HEADS = 8
HEAD_W = 128
DA_HEAD_DIM = 64
CHUNK = 64
CONV_K = 4
N_BUCKETS = 32
MAX_DISTANCE = 128
NORM_EPS = 1e-6
SUBLN_EPS = 1e-5
HEAD_LN_EPS = 1e-5
NEG_INF = -1e30
LOG2E = 1.4426950408889634

COL_DA_Q, COL_DA_K, COL_DA_V, COL_DA_Z = 0, 1, 2, 3
COL_ML_Q, COL_ML_K, COL_ML_V, COL_ML_O, COL_ML_Z = 4, 5, 6, 7, 8
COL_GATE_A, COL_GATE_M = 9, 10
N_COL_BLOCKS = 11

ATT_BLOCK = 256
ML_CHUNK = 128
VMEM_LIMIT = 56 * 1024 * 1024


def _sigmoid(x):
    return 1.0 / (1.0 + jnp.exp(-x))


def _dot(a, b):
    return jnp.dot(a, b, preferred_element_type=F32)


def _dot_nt(a, b):
    return lax.dot_general(a, b, (((1,), (1,)), ((), ())), preferred_element_type=F32)


def _dot_tn(a, b):
    return lax.dot_general(a, b, (((0,), (0,)), ((), ())), preferred_element_type=F32)


def _inproj_kernel(x_ref, nw_ref, w_ref, wg_ref, o_ref, g_ref, xn_ref):
    @pl.when(pl.program_id(1) == 0)
    def _():
        x = x_ref[...]
        ms = jnp.mean(x * x, axis=-1, keepdims=True)
        xn = (x * lax.rsqrt(ms + NORM_EPS) * nw_ref[...]).astype(BF16)
        xn_ref[...] = xn
        g_ref[...] = _dot(xn, wg_ref[...])

    o_ref[...] = _dot(xn_ref[...], w_ref[...]).astype(BF16)


def _inproj(x2, norm_w, w_big, w_gate, tm, tn):
    m = x2.shape[0]
    n = w_big.shape[1]
    return pl.pallas_call(
        _inproj_kernel,
        grid=(m // tm, n // tn),
        in_specs=[
            pl.BlockSpec((tm, D_MODEL), lambda i, j: (i, 0)),
            pl.BlockSpec((1, D_MODEL), lambda i, j: (0, 0)),
            pl.BlockSpec((D_MODEL, tn), lambda i, j: (0, j)),
            pl.BlockSpec((D_MODEL, 2 * HEAD_W), lambda i, j: (0, 0)),
        ],
        out_specs=[
            pl.BlockSpec((tm, tn), lambda i, j: (i, j)),
            pl.BlockSpec((tm, 2 * HEAD_W), lambda i, j: (i, 0)),
        ],
        out_shape=[
            jax.ShapeDtypeStruct((m, n), BF16),
            jax.ShapeDtypeStruct((m, 2 * HEAD_W), F32),
        ],
        scratch_shapes=[pltpu.VMEM((tm, D_MODEL), BF16)],
        compiler_params=pltpu.CompilerParams(
            dimension_semantics=("arbitrary", "arbitrary"), vmem_limit_bytes=VMEM_LIMIT),
        name="inproj",
    )(x2, norm_w, w_big, w_gate)


def _rel_bucket_np(rel):
    nb = N_BUCKETS // 2
    max_exact = nb // 2
    bucket = np.where(rel > 0, nb, 0)
    n = np.abs(rel)
    nf = np.maximum(n, 1).astype(np.float32)
    large = max_exact + (np.log(nf / np.float32(max_exact)) / np.float32(math.log(MAX_DISTANCE / max_exact))
                         * np.float32(nb - max_exact)).astype(np.int32)
    large = np.minimum(large, nb - 1)
    return bucket + np.where(n < max_exact, n, large)


def _bucket_tiles():
    dk = np.arange(ATT_BLOCK)[:, None]
    dq = np.arange(ATT_BLOCK)[None, :]
    diag = np.where(dk // CHUNK <= dq // CHUNK, _rel_bucket_np(dk - dq), -1)
    prev = _rel_bucket_np(dk - dq - ATT_BLOCK)
    return np.stack([diag, prev]).astype(np.int32)


FAR_BUCKET = int(_rel_bucket_np(np.array([-(ATT_BLOCK + 1)]))[0])


def _bias_kernel(rb_ref, bkt_ref, o_ref):
    h = pl.program_id(0)
    c_far = rb_ref[FAR_BUCKET, h]
    for t in range(2):
        bk = bkt_ref[t]
        acc = jnp.full((ATT_BLOCK, ATT_BLOCK), NEG_INF, F32)
        for n in range(N_BUCKETS):
            acc = jnp.where(bk == n, (rb_ref[n, h] - c_far) * LOG2E, acc)
        o_ref[0, t] = acc


def _bias_tiles(rel_bias):
    return pl.pallas_call(
        _bias_kernel,
        grid=(N_HEADS,),
        in_specs=[
            pl.BlockSpec(memory_space=pltpu.SMEM),
            pl.BlockSpec((2, ATT_BLOCK, ATT_BLOCK), lambda h: (0, 0, 0)),
        ],
        out_specs=pl.BlockSpec((1, 2, ATT_BLOCK, ATT_BLOCK), lambda h: (h, 0, 0, 0)),
        out_shape=jax.ShapeDtypeStruct((N_HEADS, 2, ATT_BLOCK, ATT_BLOCK), F32),
        name="bias_tiles",
    )(rel_bias, jnp.asarray(_bucket_tiles()))


def _attn_kernel(lam_ref, subln_ref, bias_ref, q_ref, k_ref, v_ref, z_ref, o_ref,
                 vt_ref, s_ref, *, seq, lambda_init):
    tb = ATT_BLOCK
    lamv = lam_ref[...]
    lam = (jnp.exp(jnp.sum(lamv[0:1] * lamv[1:2], axis=-1, keepdims=True))
           - jnp.exp(jnp.sum(lamv[2:3] * lamv[3:4], axis=-1, keepdims=True)) + lambda_init)

    for i in range(seq // HEAD_W):
        sl = slice(i * HEAD_W, (i + 1) * HEAD_W)
        vt_ref[:, sl] = v_ref[0, sl, :].astype(F32).T.astype(BF16)

    lane = lax.broadcasted_iota(jnp.int32, (tb, HEAD_W), 1)

    for qi in range(seq // tb):
        q0 = qi * tb
        q = (q_ref[0, q0:q0 + tb, :].astype(F32) * (DA_HEAD_DIM ** -0.5 * LOG2E)).astype(BF16)
        q_halves = (jnp.where(lane < DA_HEAD_DIM, q, jnp.zeros_like(q)),
                    jnp.where(lane >= DA_HEAD_DIM, q, jnp.zeros_like(q)))
        outs = []
        for p in range(2):
            m = None
            for j in range(qi + 1):
                s = _dot_nt(k_ref[0, j * tb:(j + 1) * tb, :], q_halves[p])
                if j >= qi - 1:
                    s = s + bias_ref[0, qi - j]
                s_ref[p, j * tb:(j + 1) * tb, :] = s
                bm = jnp.max(s, axis=0, keepdims=True)
                m = bm if m is None else jnp.maximum(m, bm)
            l = None
            acc = None
            for j in range(qi + 1):
                pt = jnp.exp2(s_ref[p, j * tb:(j + 1) * tb, :] - m)
                bl = jnp.sum(pt, axis=0, keepdims=True)
                ba = _dot(vt_ref[:, j * tb:(j + 1) * tb], pt.astype(BF16))
                l = bl if l is None else l + bl
                acc = ba if acc is None else acc + ba
            outs.append(acc * (1.0 / l))
        o = outs[0] - lam * outs[1]
        ms = jnp.mean(o * o, axis=0, keepdims=True)
        on = (o * lax.rsqrt(ms + SUBLN_EPS)).T
        z = z_ref[0, q0:q0 + tb, :].astype(F32)
        y = on * subln_ref[...] * (1.0 - lambda_init) * (z * _sigmoid(z))
        o_ref[0, q0:q0 + tb, :] = y.astype(o_ref.dtype)


def _attention(proj3, bias_tiles, lam, subln_w, lambda_init):
    b, s, _ = proj3.shape

    def col(block):
        return pl.BlockSpec((1, s, HEAD_W), lambda bi, hi: (bi, 0, block * N_HEADS + hi))

    return pl.pallas_call(
        functools.partial(_attn_kernel, seq=s, lambda_init=lambda_init),
        grid=(b, N_HEADS),
        in_specs=[
            pl.BlockSpec((4, DA_HEAD_DIM), lambda bi, hi: (0, 0)),
            pl.BlockSpec((1, HEAD_W), lambda bi, hi: (0, 0)),
            pl.BlockSpec((1, 2, ATT_BLOCK, ATT_BLOCK), lambda bi, hi: (hi, 0, 0, 0)),
            col(COL_DA_Q), col(COL_DA_K), col(COL_DA_V), col(COL_DA_Z),
        ],
        out_specs=pl.BlockSpec((1, s, HEAD_W), lambda bi, hi: (bi, 0, hi)),
        out_shape=jax.ShapeDtypeStruct((b, s, D_MODEL), BF16),
        scratch_shapes=[
            pltpu.VMEM((HEAD_W, s), BF16),
            pltpu.VMEM((2, s, ATT_BLOCK), F32),
        ],
        compiler_params=pltpu.CompilerParams(
            dimension_semantics=("arbitrary", "arbitrary"), vmem_limit_bytes=VMEM_LIMIT),
        name="diff_attention",
    )(lam, subln_w, bias_tiles, proj3, proj3, proj3, proj3)


_BCAST_PARTS = (3, 3, 2, 2, 2)


def _bcast_select_np():
    n_out = len(_BCAST_PARTS)
    e = np.zeros((N_HEADS, HEAD_W, n_out * HEAD_W), np.float32)
    g = 0
    for o, parts in enumerate(_BCAST_PARTS):
        for _ in range(parts):
            for h in range(N_HEADS):
                e[h, 8 * g + h, o * HEAD_W:(o + 1) * HEAD_W] = 1.0
            g += 1
    return e


def _split_parts(x, n):
    parts = []
    r = x
    for i in range(n):
        p = r.astype(BF16).astype(F32)
        parts.append(p)
        if i + 1 < n:
            r = r - p
    return parts


def _mlstm_kernel(q_ref, k_ref, v_ref, o_ref, z_ref, g_ref, cw_ref, cb_ref, bif_ref, mhw_ref,
                  ltri_ref, esel_ref, y_ref, qx_ref, kx_ref, qc_ref, kc_ref, c_ref, m_ref, *, tblk):
    L = ML_CHUNK

    @pl.when(pl.program_id(1) == 0)
    def _():
        qx_ref[0:8, :] = jnp.zeros((8, D_MODEL), F32)
        kx_ref[0:8, :] = jnp.zeros((8, D_MODEL), F32)
        c_ref[...] = jnp.zeros_like(c_ref)
        m_ref[...] = jnp.zeros_like(m_ref)

    qx_ref[8:, :] = q_ref[0].astype(F32)
    kx_ref[8:, :] = k_ref[0].astype(F32)
    accq = cb_ref[0:1, :]
    acck = cb_ref[1:2, :]
    for j in range(CONV_K):
        accq = accq + cw_ref[0, j:j + 1, :] * qx_ref[pl.ds(8 - (CONV_K - 1) + j, tblk), :]
        acck = acck + cw_ref[1, j:j + 1, :] * kx_ref[pl.ds(8 - (CONV_K - 1) + j, tblk), :]
    qc_ref[...] = accq * _sigmoid(accq)
    kc_ref[...] = (acck * _sigmoid(acck) * HEAD_W ** -0.5).astype(BF16)
    qx_ref[0:8, :] = qx_ref[tblk:tblk + 8, :]
    kx_ref[0:8, :] = kx_ref[tblk:tblk + 8, :]

    row = lax.broadcasted_iota(jnp.int32, (L, HEAD_W), 0)
    colv = lax.broadcasted_iota(jnp.int32, (L, HEAD_W), 1)
    causal = colv <= row
    group = colv // 8
    ones_blk = jnp.ones((L, HEAD_W), BF16)
    ltri = ltri_ref[...]

    def chunk(c, _):
        r0 = pl.multiple_of(c * L, L)
        rows = pl.ds(r0, L)
        m_state = m_ref[...]
        ig = g_ref[0, rows, 0:HEAD_W] + bif_ref[0:1, :]
        fg = g_ref[0, rows, HEAD_W:2 * HEAD_W] + bif_ref[1:2, :]
        lf = jnp.minimum(fg, 0.0) - jnp.log(1.0 + jnp.exp(-jnp.abs(fg)))
        bcum = sum(_dot(ltri, p.astype(BF16)) for p in _split_parts(lf, 3))
        wp = ig - bcum
        cmax = wp
        sh = 1
        while sh < L:
            cmax = jnp.where(row >= sh, jnp.maximum(cmax, pltpu.roll(cmax, sh, 0)), cmax)
            sh *= 2
        inter = bcum + m_state
        m_t = jnp.maximum(inter, bcum + cmax)
        u = bcum - m_t
        w_inter = jnp.exp(inter - m_t)
        b_last = bcum[L - 1:L, :]
        w_log = b_last + wp
        m_new = jnp.maximum(b_last + m_state, jnp.max(w_log, axis=0, keepdims=True))
        w_state = jnp.exp(w_log - m_new)
        decay = jnp.broadcast_to(jnp.exp(b_last + m_state - m_new), (L, HEAD_W))
        m_ref[...] = m_new

        parts = []
        for x, n in zip((u, m_t, w_inter, w_state, decay), _BCAST_PARTS):
            parts.extend(_split_parts(x, n))
        packed = jnp.zeros((L, HEAD_W), F32)
        for gi, p in enumerate(parts):
            packed = jnp.where(group == gi, p, packed)
        packed = packed.astype(BF16)
        wp_t = wp.T

        for h in range(N_HEADS):
            hs = slice(h * HEAD_W, (h + 1) * HEAD_W)
            bc = _dot(packed, esel_ref[h])
            u_b = bc[:, 0:HEAD_W]
            m_b = bc[:, HEAD_W:2 * HEAD_W]
            wi_b = bc[:, 2 * HEAD_W:3 * HEAD_W]
            ws_b = bc[:, 3 * HEAD_W:4 * HEAD_W]
            dc_b = bc[:, 4 * HEAD_W:5 * HEAD_W]
            w_intra = jnp.exp(jnp.where(causal, u_b + wp_t[h:h + 1, :], NEG_INF))
            qf = qc_ref[rows, hs]
            kb = kc_ref[rows, hs]
            vb = v_ref[0, rows, hs]
            scores = (_dot_nt(qf.astype(BF16), kb) * w_intra).astype(BF16)
            lhs = jnp.concatenate([scores, (qf * wi_b).astype(BF16)], axis=1)
            c_old = c_ref[h]
            rhs = jnp.concatenate(
                [jnp.concatenate([vb, ones_blk], axis=1), c_old.astype(BF16)], axis=0)
            tot = _dot(lhs, rhs)
            hh = tot[:, 0:HEAD_W] / jnp.maximum(jnp.abs(tot[:, HEAD_W:]), jnp.exp(-m_b))
            mu = jnp.mean(hh, axis=-1, keepdims=True)
            hc = hh - mu
            var = jnp.mean(hc * hc, axis=-1, keepdims=True)
            hn = hc * lax.rsqrt(var + HEAD_LN_EPS) * mhw_ref[:, hs]
            og = o_ref[0, rows, hs].astype(F32)
            zg = z_ref[0, rows, hs].astype(F32)
            y_ref[0, rows, hs] = ((_sigmoid(og) * hn) * (zg * _sigmoid(zg))).astype(y_ref.dtype)

            wv = jnp.concatenate([(ws_b * vb.astype(F32)).astype(BF16), ws_b.astype(BF16)], axis=1)
            upd = _dot_tn(kb, wv)
            c_ref[h] = jnp.concatenate([dc_b, dc_b], axis=1) * c_old + upd
        return 0

    lax.fori_loop(0, tblk // L, chunk, 0)


def _mlstm(proj3, gates3, conv_w, conv_b, bif, mh_w, tblk):
    b, s, _ = proj3.shape

    def col(block):
        return pl.BlockSpec((1, tblk, D_MODEL), lambda bi, ti: (bi, ti, block))

    def full(shape):
        return pl.BlockSpec(shape, lambda bi, ti: (0,) * len(shape))

    ltri = jnp.asarray(np.tril(np.ones((ML_CHUNK, ML_CHUNK), np.float32)), BF16)
    esel = jnp.asarray(_bcast_select_np(), BF16)
    return pl.pallas_call(
        functools.partial(_mlstm_kernel, tblk=tblk),
        grid=(b, s // tblk),
        in_specs=[
            col(COL_ML_Q), col(COL_ML_K), col(COL_ML_V), col(COL_ML_O), col(COL_ML_Z),
            pl.BlockSpec((1, tblk, 2 * HEAD_W), lambda bi, ti: (bi, ti, 0)),
            full((2, CONV_K, D_MODEL)), full((2, D_MODEL)), full((2, HEAD_W)), full((1, D_MODEL)),
            full((ML_CHUNK, ML_CHUNK)), full(esel.shape),
        ],
        out_specs=pl.BlockSpec((1, tblk, D_MODEL), lambda bi, ti: (bi, ti, 0)),
        out_shape=jax.ShapeDtypeStruct((b, s, D_MODEL), BF16),
        scratch_shapes=[
            pltpu.VMEM((tblk + 8, D_MODEL), F32),
            pltpu.VMEM((tblk + 8, D_MODEL), F32),
            pltpu.VMEM((tblk, D_MODEL), F32),
            pltpu.VMEM((tblk, D_MODEL), BF16),
            pltpu.VMEM((N_HEADS, HEAD_W, 2 * HEAD_W), F32),
            pltpu.VMEM((1, HEAD_W), F32),
        ],
        compiler_params=pltpu.CompilerParams(
            dimension_semantics=("arbitrary", "arbitrary"), vmem_limit_bytes=VMEM_LIMIT),
        name="mlstm",
    )(proj3, proj3, proj3, proj3, proj3, gates3, conv_w, conv_b, bif, mh_w, ltri, esel)


def _out_kernel(ya_ref, ym_ref, ga_ref, gm_ref, x_ref, wpa_ref, wpm_ref, wout_ref, bg_ref, nf_ref,
                o_ref, *, final_norm):
    g_a = _sigmoid(ga_ref[...].astype(F32) + bg_ref[0:1, :])
    g_m = _sigmoid(gm_ref[...].astype(F32) + bg_ref[1:2, :])
    merged = g_a * _dot(ya_ref[...], wpa_ref[...]) + g_m * _dot(ym_ref[...], wpm_ref[...])
    hres = x_ref[...] + _dot(merged.astype(BF16), wout_ref[...])
    if final_norm:
        ms = jnp.mean(hres * hres, axis=-1, keepdims=True)
        hres = hres * lax.rsqrt(ms + NORM_EPS) * nf_ref[...]
    o_ref[...] = hres


def _out_stage(y_a, y_m, proj, x2, w_pa, w_pm, w_out, b_gate, norm_final, tm, final_norm):
    m = x2.shape[0]

    def rows(block=0):
        return pl.BlockSpec((tm, D_MODEL), lambda i: (i, block))

    def full(shape):
        return pl.BlockSpec(shape, lambda i: (0,) * len(shape))

    return pl.pallas_call(
        functools.partial(_out_kernel, final_norm=final_norm),
        grid=(m // tm,),
        in_specs=[
            rows(), rows(), rows(COL_GATE_A), rows(COL_GATE_M), rows(),
            full((D_MODEL, D_MODEL)), full((D_MODEL, D_MODEL)), full((D_MODEL, D_MODEL)),
            full((2, D_MODEL)), full((1, D_MODEL)),
        ],
        out_specs=rows(),
        out_shape=jax.ShapeDtypeStruct((m, D_MODEL), F32),
        compiler_params=pltpu.CompilerParams(
            dimension_semantics=("arbitrary",), vmem_limit_bytes=VMEM_LIMIT),
        name="out_stage",
    )(y_a, y_m, proj, proj, x2, w_pa, w_pm, w_out, b_gate, norm_final)


def _regroup_in_weight(w):
    n_wide = 9 * D_MODEL
    w_big = jnp.concatenate([w[:, :n_wide], w[:, n_wide + 2 * N_HEADS:]], axis=1).astype(BF16)
    w_i = w[:, n_wide:n_wide + N_HEADS]
    w_f = w[:, n_wide + N_HEADS:n_wide + 2 * N_HEADS]
    reps = HEAD_W // N_HEADS
    w_gate = jnp.concatenate([jnp.tile(w_i, (1, reps)), jnp.tile(w_f, (1, reps))], axis=1).astype(BF16)
    return w_big, w_gate


def kernel(x, norm_w, w_in, lam, subln_w, rel_bias, conv_w, conv_b, b_if, mh_w, b_gate, w_pa, w_pm, w_out,
           norm_final):
    b, s, d = x.shape
    depth = norm_w.shape[0]
    m = b * s
    tm_in = min(1024, m)
    tm_out = min(512, m)
    tblk = min(512, s)
    bias_tiles = _bias_tiles(rel_bias)
    h2 = x.reshape(m, d)
    for layer in range(depth):
        lambda_init = 0.8 - 0.6 * math.exp(-0.3 * layer)
        w_big, w_gate = _regroup_in_weight(w_in[layer])
        proj, gates = _inproj(h2, norm_w[layer][None, :], w_big, w_gate, tm_in, D_MODEL)
        proj3 = proj.reshape(b, s, N_COL_BLOCKS * D_MODEL)
        y_a = _attention(proj3, bias_tiles, lam[layer], subln_w[layer][None, :], lambda_init)
        bif = jnp.tile(b_if[layer], (1, HEAD_W // N_HEADS))
        y_m = _mlstm(proj3, gates.reshape(b, s, 2 * HEAD_W), conv_w[layer], conv_b[layer], bif,
                     mh_w[layer][None, :], tblk)
        h2 = _out_stage(y_a.reshape(m, d), y_m.reshape(m, d), proj, h2,
                        w_pa[layer].astype(BF16), w_pm[layer].astype(BF16), w_out[layer].astype(BF16),
                        b_gate[layer], norm_final[None, :], tm_out, final_norm=(layer == depth - 1))
    return h2.reshape(b, s, d)
```

```python
import functools
import math

import numpy as np
import jax
import jax.numpy as jnp
from jax import lax
from jax.experimental import pallas as pl
from jax.experimental.pallas import tpu as pltpu

F32 = jnp.float32
BF16 = jnp.bfloat16

D_MODEL = 1024
N_HEADS = 8
HEAD_W = 128
DA_HEAD_DIM = 64
CHUNK = 64
CONV_K = 4
N_BUCKETS = 32
MAX_DISTANCE = 128
NORM_EPS = 1e-6
SUBLN_EPS = 1e-5
HEAD_LN_EPS = 1e-5
NEG_INF = -1e30
LOG2E = 1.4426950408889634

COL_DA_Q, COL_DA_K, COL_DA_V, COL_DA_Z = 0, 1, 2, 3
COL_ML_Q, COL_ML_K, COL_ML_V, COL_ML_O, COL_ML_Z = 4, 5, 6, 7, 8
COL_GATE_A, COL_GATE_M = 9, 10
N_COL_BLOCKS = 11

ATT_BLOCK = 256
ONES_ROWS = 16
ML_CHUNK = 128
HEAD_PIPE = 2
VMEM_LIMIT = 56 * 1024 * 1024


def _sigmoid(x):
    return 1.0 / (1.0 + jnp.exp(-x))


def _dot(a, b):
    return jnp.dot(a, b, preferred_element_type=F32)


def _dot_nt(a, b):
    return lax.dot_general(a, b, (((1,), (1,)), ((), ())), preferred_element_type=F32)


def _dot_tn(a, b):
    return lax.dot_general(a, b, (((0,), (0,)), ((), ())), preferred_element_type=F32)


CONV_ROWS = 512
CONV_PAD = 8


def _inproj_kernel(x_ref, nw_ref, w_ref, wg_ref, cw_ref, cb_ref, o_ref, g_ref, xn_ref, y_ref, *, seq):
    j = pl.program_id(1)

    @pl.when(j == 0)
    def _():
        x = x_ref[...]
        ms = jnp.mean(x * x, axis=-1, keepdims=True)
        xn = (x * lax.rsqrt(ms + NORM_EPS) * nw_ref[...]).astype(BF16)
        xn_ref[...] = xn
        g_ref[...] = _dot(xn, wg_ref[...])
        y_ref[0:CONV_PAD, :] = jnp.zeros((CONV_PAD, D_MODEL), F32)

    is_conv = (j == COL_ML_Q) | (j == COL_ML_K)

    @pl.when(jnp.logical_not(is_conv))
    def _():
        o_ref[...] = _dot(xn_ref[...], w_ref[...]).astype(BF16)

    @pl.when(is_conv)
    def _():
        which = j - COL_ML_Q
        taps = cw_ref[which]
        bias = cb_ref[pl.ds(which, 1), :]
        scale = jnp.where(which == 0, 1.0, HEAD_W ** -0.5)
        def project(r):
            y_ref[CONV_PAD + r:CONV_PAD + r + CONV_ROWS, :] = _dot(xn_ref[r:r + CONV_ROWS, :], w_ref[...])

        def conv(r):
            win = y_ref[r:r + CONV_ROWS + CONV_PAD, :]
            acc = bias + taps[CONV_K - 1:CONV_K, :] * win[CONV_PAD:, :]
            for back in range(1, CONV_K):
                shifted = pltpu.roll(win, back, 0)[CONV_PAD:, :]
                acc = acc + taps[CONV_K - 1 - back:CONV_K - back, :] * shifted
            o_ref[r:r + CONV_ROWS, :] = (acc * _sigmoid(acc) * scale).astype(BF16)

        project(0)
        for r in range(0, seq, CONV_ROWS):
            if r + CONV_ROWS < seq:
                project(r + CONV_ROWS)
            conv(r)


def _inproj(x2, norm_w, w_big, w_gate, conv_w, conv_b, seq):
    m = x2.shape[0]
    n = w_big.shape[1]
    return pl.pallas_call(
        functools.partial(_inproj_kernel, seq=seq),
        grid=(m // seq, n // D_MODEL),
        in_specs=[
            pl.BlockSpec((seq, D_MODEL), lambda i, j: (i, 0)),
            pl.BlockSpec((1, D_MODEL), lambda i, j: (0, 0)),
            pl.BlockSpec((D_MODEL, D_MODEL), lambda i, j: (0, j)),
            pl.BlockSpec((D_MODEL, 2 * HEAD_W), lambda i, j: (0, 0)),
            pl.BlockSpec((2, CONV_K, D_MODEL), lambda i, j: (0, 0, 0)),
            pl.BlockSpec((2, D_MODEL), lambda i, j: (0, 0)),
        ],
        out_specs=[
            pl.BlockSpec((seq, D_MODEL), lambda i, j: (i, j)),
            pl.BlockSpec((seq, 2 * HEAD_W), lambda i, j: (i, 0)),
        ],
        out_shape=[
            jax.ShapeDtypeStruct((m, n), BF16),
            jax.ShapeDtypeStruct((m, 2 * HEAD_W), F32),
        ],
        scratch_shapes=[
            pltpu.VMEM((seq, D_MODEL), BF16),
            pltpu.VMEM((seq + CONV_PAD, D_MODEL), F32),
        ],
        compiler_params=pltpu.CompilerParams(
            dimension_semantics=("arbitrary", "arbitrary"), vmem_limit_bytes=VMEM_LIMIT),
        name="inproj",
    )(x2, norm_w, w_big, w_gate, conv_w, conv_b)


def _rel_bucket_np(rel):
    nb = N_BUCKETS // 2
    max_exact = nb // 2
    bucket = np.where(rel > 0, nb, 0)
    n = np.abs(rel)
    nf = np.maximum(n, 1).astype(np.float32)
    large = max_exact + (np.log(nf / np.float32(max_exact)) / np.float32(math.log(MAX_DISTANCE / max_exact))
                         * np.float32(nb - max_exact)).astype(np.int32)
    large = np.minimum(large, nb - 1)
    return bucket + np.where(n < max_exact, n, large)


def _bucket_tiles():
    dk = np.arange(ATT_BLOCK)[:, None]
    dq = np.arange(ATT_BLOCK)[None, :]
    diag = np.where(dk // CHUNK <= dq // CHUNK, _rel_bucket_np(dk - dq), -1)
    prev = _rel_bucket_np(dk - dq - ATT_BLOCK)
    return np.stack([diag, prev]).astype(np.int32)


FAR_BUCKET = int(_rel_bucket_np(np.array([-(ATT_BLOCK + 1)]))[0])


def _bias_kernel(rb_ref, bkt_ref, o_ref):
    h = pl.program_id(0)
    c_far = rb_ref[FAR_BUCKET, h]
    for t in range(2):
        bk = bkt_ref[t]
        acc = jnp.full((ATT_BLOCK, ATT_BLOCK), NEG_INF, F32)
        for n in range(N_BUCKETS):
            acc = jnp.where(bk == n, (rb_ref[n, h] - c_far) * LOG2E, acc)
        o_ref[0, t] = acc


def _bias_tiles(rel_bias):
    return pl.pallas_call(
        _bias_kernel,
        grid=(N_HEADS,),
        in_specs=[
            pl.BlockSpec(memory_space=pltpu.SMEM),
            pl.BlockSpec((2, ATT_BLOCK, ATT_BLOCK), lambda h: (0, 0, 0)),
        ],
        out_specs=pl.BlockSpec((1, 2, ATT_BLOCK, ATT_BLOCK), lambda h: (h, 0, 0, 0)),
        out_shape=jax.ShapeDtypeStruct((N_HEADS, 2, ATT_BLOCK, ATT_BLOCK), F32),
        name="bias_tiles",
    )(rel_bias, jnp.asarray(_bucket_tiles()))


def _attn_kernel(lam_ref, subln_ref, bias_ref, q_ref, k_ref, v_ref, z_ref, o_ref,
                 vt_ref, s_ref, *, seq, lambda_init):
    tb = ATT_BLOCK
    lamv = lam_ref[...]
    lam = (jnp.exp(jnp.sum(lamv[0:1] * lamv[1:2], axis=-1, keepdims=True))
           - jnp.exp(jnp.sum(lamv[2:3] * lamv[3:4], axis=-1, keepdims=True)) + lambda_init)

    for i in range(seq // HEAD_W):
        sl = slice(i * HEAD_W, (i + 1) * HEAD_W)
        vt_ref[0:HEAD_W, sl] = v_ref[0, sl, :].astype(F32).T.astype(BF16)
    vt_ref[HEAD_W:, :] = jnp.ones((ONES_ROWS, seq), BF16)

    lane = lax.broadcasted_iota(jnp.int32, (tb, HEAD_W), 1)
    blocks = lambda j: slice(j * tb, (j + 1) * tb)
    q_halves, col_max, normed = {}, {}, {}

    def logits_tasks(qi, p):
        def load_q():
            if qi not in q_halves:
                q = (q_ref[0, blocks(qi), :].astype(F32) * (DA_HEAD_DIM ** -0.5 * LOG2E)).astype(BF16)
                q_halves[qi] = (jnp.where(lane < DA_HEAD_DIM, q, jnp.zeros_like(q)),
                                jnp.where(lane >= DA_HEAD_DIM, q, jnp.zeros_like(q)))

        def task(j):
            load_q()
            s = _dot_nt(k_ref[0, blocks(j), :], q_halves[qi][p])
            if j >= qi - 1:
                s = s + bias_ref[0, qi - j]
            s_ref[p, blocks(j), :] = s

        return [functools.partial(task, j) for j in range(qi + 1)]

    def finish_logits(qi, p):
        m = None
        for j in range(qi + 1):
            bm = jnp.max(s_ref[p, blocks(j), :].reshape(tb // 8, 8, tb), axis=0)
            m = bm if m is None else jnp.maximum(m, bm)
        col_max[qi, p] = jnp.max(m, axis=0, keepdims=True)

    def probs_tasks(qi, p):
        acc = []

        def task(j):
            pt = jnp.exp2(s_ref[p, blocks(j), :] - col_max[qi, p]).astype(BF16)
            ba = _dot(vt_ref[:, blocks(j)], pt)
            acc[:] = [ba if not acc else acc[0] + ba]

        def finish():
            normed[qi, p] = acc[0][0:HEAD_W] * (1.0 / acc[0][HEAD_W:HEAD_W + 1])

        return [functools.partial(task, j) for j in range(qi + 1)], finish

    def epilogue(qi):
        o = normed[qi, 0] - lam * normed[qi, 1]
        ms = jnp.mean(o * o, axis=0, keepdims=True)
        on = (o * lax.rsqrt(ms + SUBLN_EPS)).T
        z = z_ref[0, blocks(qi), :].astype(F32)
        y = on * subln_ref[...] * (1.0 - lambda_init) * (z * _sigmoid(z))
        o_ref[0, blocks(qi), :] = y.astype(o_ref.dtype)

    units = [(qi, p) for qi in range(seq // tb) for p in range(2)]
    pending = None
    for unit in units + [None]:
        first = logits_tasks(*unit) if unit is not None else []
        second, finish_probs = probs_tasks(*pending) if pending is not None else ([], None)
        for i in range(max(len(first), len(second))):
            if i < len(first):
                first[i]()
            if i < len(second):
                second[i]()
        if unit is not None:
            finish_logits(*unit)
        if pending is not None:
            finish_probs()
            if pending[1] == 1:
                epilogue(pending[0])
        pending = unit


def _attention(proj3, bias_tiles, lam, subln_w, lambda_init):
    b, s, _ = proj3.shape

    def col(block):
        return pl.BlockSpec((1, s, HEAD_W), lambda bi, hi: (bi, 0, block * N_HEADS + hi))

    return pl.pallas_call(
        functools.partial(_attn_kernel, seq=s, lambda_init=lambda_init),
        grid=(b, N_HEADS),
        in_specs=[
            pl.BlockSpec((4, DA_HEAD_DIM), lambda bi, hi: (0, 0)),
            pl.BlockSpec((1, HEAD_W), lambda bi, hi: (0, 0)),
            pl.BlockSpec((1, 2, ATT_BLOCK, ATT_BLOCK), lambda bi, hi: (hi, 0, 0, 0)),
            col(COL_DA_Q), col(COL_DA_K), col(COL_DA_V), col(COL_DA_Z),
        ],
        out_specs=pl.BlockSpec((1, s, HEAD_W), lambda bi, hi: (bi, 0, hi)),
        out_shape=jax.ShapeDtypeStruct((b, s, D_MODEL), BF16),
        scratch_shapes=[
            pltpu.VMEM((HEAD_W + ONES_ROWS, s), BF16),
            pltpu.VMEM((2, s, ATT_BLOCK), F32),
        ],
        compiler_params=pltpu.CompilerParams(
            dimension_semantics=("arbitrary", "arbitrary"), vmem_limit_bytes=VMEM_LIMIT),
        name="diff_attention",
    )(lam, subln_w, bias_tiles, proj3, proj3, proj3, proj3)


_BCAST_PARTS = (3, 3, 2, 2)
_DECAY_PARTS = 2


_BCAST_W = len(_BCAST_PARTS) * HEAD_W


def _bcast_select_np():
    e = np.zeros((HEAD_W, N_HEADS * _BCAST_W), np.float32)
    g = 0
    for o, parts in enumerate(_BCAST_PARTS):
        for _ in range(parts):
            for h in range(N_HEADS):
                e[8 * g + h, h * _BCAST_W + o * HEAD_W:h * _BCAST_W + (o + 1) * HEAD_W] = 1.0
            g += 1
    return e


def _row_select_np():
    e = np.zeros((HEAD_W, N_HEADS * HEAD_W), np.float32)
    for g in range(_DECAY_PARTS):
        for h in range(N_HEADS):
            e[8 * g + h, h * HEAD_W:(h + 1) * HEAD_W] = 1.0
    return e


def _split_parts(x, n):
    parts = []
    r = x
    for i in range(n):
        p = r.astype(BF16).astype(F32)
        parts.append(p)
        if i + 1 < n:
            r = r - p
    return parts


def _pack_groups(arrays_and_parts, group):
    parts = []
    for x, n in arrays_and_parts:
        parts.extend(_split_parts(x, n))
    packed = jnp.zeros_like(parts[0])
    for gi, p in enumerate(parts):
        packed = jnp.where(group == gi, p, packed)
    return packed.astype(BF16)


def _mlstm_kernel(q_ref, k_ref, v_ref, g_ref, bif_ref, mhw_ref, ltri_ref, esel_ref, rsel_ref, y_ref,
                  bc_ref, wpt_ref, dcrow_ref, c_ref, m_ref, *, tblk):
    L = ML_CHUNK
    n_chunks = tblk // L

    @pl.when(pl.program_id(1) == 0)
    def _():
        c_ref[...] = jnp.zeros_like(c_ref)
        m_ref[...] = jnp.zeros_like(m_ref)

    row = lax.broadcasted_iota(jnp.int32, (tblk, HEAD_W), 0) % L
    group = lax.broadcasted_iota(jnp.int32, (tblk, HEAD_W), 1) // 8
    ig = g_ref[0, :, 0:HEAD_W] + bif_ref[0:1, :]
    fg = g_ref[0, :, HEAD_W:2 * HEAD_W] + bif_ref[1:2, :]
    lf = jnp.minimum(fg, 0.0) - jnp.log(1.0 + jnp.exp(-jnp.abs(fg)))
    ltri = ltri_ref[...]
    lf_parts = [p.astype(BF16) for p in _split_parts(lf, 3)]
    bcum = jnp.concatenate(
        [sum(_dot(ltri, p[c * L:(c + 1) * L]) for p in lf_parts) for c in range(n_chunks)], axis=0)
    wp = ig - bcum
    cmax = wp
    sh = 1
    while sh < L:
        cmax = jnp.where(row >= sh, jnp.maximum(cmax, pltpu.roll(cmax, sh, 0)), cmax)
        sh *= 2
    m_rows, b_last_rows, m_next_rows = [], [], []
    m_state = m_ref[...]
    for c in range(n_chunks):
        last = (c + 1) * L - 1
        b_last = bcum[last:last + 1, :]
        m_next = b_last + jnp.maximum(m_state, cmax[last:last + 1, :])
        dcrow_ref[c] = _dot(
            _pack_groups([(jnp.broadcast_to(jnp.exp(b_last + m_state - m_next), (8, HEAD_W)), _DECAY_PARTS)],
                         group[0:8]), rsel_ref[...])
        m_rows.append(jnp.broadcast_to(m_state, (L, HEAD_W)))
        b_last_rows.append(jnp.broadcast_to(b_last, (L, HEAD_W)))
        m_next_rows.append(jnp.broadcast_to(m_next, (L, HEAD_W)))
        m_state = m_next
    m_ref[...] = m_state
    m_prev = jnp.concatenate(m_rows, axis=0)
    inter = bcum + m_prev
    m_t = jnp.maximum(inter, bcum + cmax)
    w_inter = jnp.exp(inter - m_t)
    w_state = jnp.exp(jnp.concatenate(b_last_rows, axis=0) + wp - jnp.concatenate(m_next_rows, axis=0))
    bc_ref[...] = _dot(_pack_groups(
        [((bcum - m_t) * LOG2E, 3), (m_t * -LOG2E, 3), (w_inter, 2), (w_state, 2)], group), esel_ref[...])
    wp2 = wp * LOG2E
    for c in range(n_chunks):
        wpt_ref[c] = wp2[c * L:(c + 1) * L].T

    causal = (lax.broadcasted_iota(jnp.int32, (L, HEAD_W), 1)
              <= lax.broadcasted_iota(jnp.int32, (L, HEAD_W), 0))
    ones_blk = jnp.ones((L, HEAD_W), BF16)

    def chunk(c, _):
        rows = pl.ds(pl.multiple_of(c * L, L), L)
        wp_t = wpt_ref[c]
        dc_rows = dcrow_ref[c]
        heads = [dict(hs=slice(h * HEAD_W, (h + 1) * HEAD_W)) for h in range(N_HEADS)]

        def bcast(h, o):
            return bc_ref[rows, h * _BCAST_W + o * HEAD_W:h * _BCAST_W + (o + 1) * HEAD_W]

        def stage_scores(h):
            st = heads[h]
            st["q"], st["k"], st["v"] = (ref[0, rows, st["hs"]] for ref in (q_ref, k_ref, v_ref))
            st["s"] = _dot_nt(st["q"], st["k"])
            ws_b = bcast(h, 3)
            wv = jnp.concatenate([(ws_b * st["v"].astype(F32)).astype(BF16), ws_b.astype(BF16)], axis=1)
            st["upd"] = _dot_tn(st["k"], wv)

        def stage_readout(h):
            st = heads[h]
            w_intra = jnp.exp2(jnp.where(causal, bcast(h, 0) + wp_t[h:h + 1, :], NEG_INF))
            scores = (st["s"] * w_intra).astype(BF16)
            lhs = jnp.concatenate([scores, (st["q"].astype(F32) * bcast(h, 2)).astype(BF16)], axis=1)
            st["c_old"] = c_ref[h]
            rhs = jnp.concatenate(
                [jnp.concatenate([st["v"], ones_blk], axis=1), st["c_old"].astype(BF16)], axis=0)
            st["tot"] = _dot(lhs, rhs)

        def stage_finish(h):
            st = heads[h]
            tot = st["tot"]
            hh = tot[:, 0:HEAD_W] / jnp.maximum(jnp.abs(tot[:, HEAD_W:]), jnp.exp2(bcast(h, 1)))
            mu = jnp.mean(hh, axis=-1, keepdims=True)
            hc = hh - mu
            var = jnp.mean(hc * hc, axis=-1, keepdims=True)
            y_ref[0, rows, st["hs"]] = (
                hc * lax.rsqrt(var + HEAD_LN_EPS) * mhw_ref[:, st["hs"]]).astype(y_ref.dtype)
            dc = dc_rows[0:1, st["hs"]]
            c_ref[h] = jnp.concatenate([dc, dc], axis=1) * st["c_old"] + st["upd"]
            st.clear()

        for i in range(N_HEADS + 2 * HEAD_PIPE):
            if i < N_HEADS:
                stage_scores(i)
            if 0 <= i - HEAD_PIPE < N_HEADS:
                stage_readout(i - HEAD_PIPE)
            if 0 <= i - 2 * HEAD_PIPE < N_HEADS:
                stage_finish(i - 2 * HEAD_PIPE)
        return 0

    lax.fori_loop(0, n_chunks, chunk, 0)


def _mlstm(proj3, gates3, bif, mh_w, tblk):
    b, s, _ = proj3.shape
    n_chunks = tblk // ML_CHUNK

    def col(block):
        return pl.BlockSpec((1, tblk, D_MODEL), lambda bi, ti: (bi, ti, block))

    def full(shape):
        return pl.BlockSpec(shape, lambda bi, ti: (0,) * len(shape))

    ltri = jnp.asarray(np.tril(np.ones((ML_CHUNK, ML_CHUNK), np.float32)), BF16)
    esel = jnp.asarray(_bcast_select_np(), BF16)
    rsel = jnp.asarray(_row_select_np(), BF16)
    return pl.pallas_call(
        functools.partial(_mlstm_kernel, tblk=tblk),
        grid=(b, s // tblk),
        in_specs=[
            col(COL_ML_Q), col(COL_ML_K), col(COL_ML_V),
            pl.BlockSpec((1, tblk, 2 * HEAD_W), lambda bi, ti: (bi, ti, 0)),
            full((2, HEAD_W)), full((1, D_MODEL)),
            full((ML_CHUNK, ML_CHUNK)), full(esel.shape), full(rsel.shape),
        ],
        out_specs=pl.BlockSpec((1, tblk, D_MODEL), lambda bi, ti: (bi, ti, 0)),
        out_shape=jax.ShapeDtypeStruct((b, s, D_MODEL), BF16),
        scratch_shapes=[
            pltpu.VMEM((tblk, N_HEADS * _BCAST_W), F32),
            pltpu.VMEM((n_chunks, HEAD_W, ML_CHUNK), F32),
            pltpu.VMEM((n_chunks, 8, D_MODEL), F32),
            pltpu.VMEM((N_HEADS, HEAD_W, 2 * HEAD_W), F32),
            pltpu.VMEM((1, HEAD_W), F32),
        ],
        compiler_params=pltpu.CompilerParams(
            dimension_semantics=("arbitrary", "arbitrary"), vmem_limit_bytes=VMEM_LIMIT),
        name="mlstm",
    )(proj3, proj3, proj3, gates3, bif, mh_w, ltri, esel, rsel)


def _out_kernel(ya_ref, hn_ref, og_ref, zg_ref, ga_ref, gm_ref, x_ref, wpa_ref, wpm_ref, wout_ref, bg_ref,
                nf_ref, o_ref, *, final_norm):
    zg = zg_ref[...].astype(F32)
    y_m = ((_sigmoid(og_ref[...].astype(F32)) * hn_ref[...].astype(F32)) * (zg * _sigmoid(zg))).astype(BF16)
    g_a = _sigmoid(ga_ref[...].astype(F32) + bg_ref[0:1, :])
    g_m = _sigmoid(gm_ref[...].astype(F32) + bg_ref[1:2, :])
    merged = g_a * _dot(ya_ref[...], wpa_ref[...]) + g_m * _dot(y_m, wpm_ref[...])
    hres = x_ref[...] + _dot(merged.astype(BF16), wout_ref[...])
    if final_norm:
        ms = jnp.mean(hres * hres, axis=-1, keepdims=True)
        hres = hres * lax.rsqrt(ms + NORM_EPS) * nf_ref[...]
    o_ref[...] = hres


def _out_stage(y_a, h_n, proj, x2, w_pa, w_pm, w_out, b_gate, norm_final, tm, final_norm):
    m = x2.shape[0]

    def rows(block=0):
        return pl.BlockSpec((tm, D_MODEL), lambda i: (i, block))

    def full(shape):
        return pl.BlockSpec(shape, lambda i: (0,) * len(shape))

    return pl.pallas_call(
        functools.partial(_out_kernel, final_norm=final_norm),
        grid=(m // tm,),
        in_specs=[
            rows(), rows(), rows(COL_ML_O), rows(COL_ML_Z), rows(COL_GATE_A), rows(COL_GATE_M), rows(),
            full((D_MODEL, D_MODEL)), full((D_MODEL, D_MODEL)), full((D_MODEL, D_MODEL)),
            full((2, D_MODEL)), full((1, D_MODEL)),
        ],
        out_specs=rows(),
        out_shape=jax.ShapeDtypeStruct((m, D_MODEL), F32),
        compiler_params=pltpu.CompilerParams(
            dimension_semantics=("arbitrary",), vmem_limit_bytes=VMEM_LIMIT),
        name="out_stage",
    )(y_a, h_n, proj, proj, proj, proj, x2, w_pa, w_pm, w_out, b_gate, norm_final)


def _regroup_in_weight(w):
    n_wide = 9 * D_MODEL
    w_big = jnp.concatenate([w[:, :n_wide], w[:, n_wide + 2 * N_HEADS:]], axis=1).astype(BF16)
    w_i = w[:, n_wide:n_wide + N_HEADS]
    w_f = w[:, n_wide + N_HEADS:n_wide + 2 * N_HEADS]
    reps = HEAD_W // N_HEADS
    w_gate = jnp.concatenate([jnp.tile(w_i, (1, reps)), jnp.tile(w_f, (1, reps))], axis=1).astype(BF16)
    return w_big, w_gate


def kernel(x, norm_w, w_in, lam, subln_w, rel_bias, conv_w, conv_b, b_if, mh_w, b_gate, w_pa, w_pm, w_out,
           norm_final):
    b, s, d = x.shape
    depth = norm_w.shape[0]
    m = b * s
    tm_out = min(512, m)
    tblk = min(512, s)
    bias_tiles = _bias_tiles(rel_bias)
    h2 = x.reshape(m, d)
    for layer in range(depth):
        lambda_init = 0.8 - 0.6 * math.exp(-0.3 * layer)
        w_big, w_gate = _regroup_in_weight(w_in[layer])
        proj, gates = _inproj(h2, norm_w[layer][None, :], w_big, w_gate, conv_w[layer], conv_b[layer], s)
        proj3 = proj.reshape(b, s, N_COL_BLOCKS * D_MODEL)
        y_a = _attention(proj3, bias_tiles, lam[layer], subln_w[layer][None, :], lambda_init)
        bif = jnp.tile(b_if[layer], (1, HEAD_W // N_HEADS))
        h_n = _mlstm(proj3, gates.reshape(b, s, 2 * HEAD_W), bif, mh_w[layer][None, :], tblk)
        h2 = _out_stage(y_a.reshape(m, d), h_n.reshape(m, d), proj, h2,
                        w_pa[layer].astype(BF16), w_pm[layer].astype(BF16), w_out[layer].astype(BF16),
                        b_gate[layer], norm_final[None, :], tm_out, final_norm=(layer == depth - 1))
    return h2.reshape(b, s, d)
```

```python
import functools
import math

import numpy as np
import jax
import jax.numpy as jnp
from jax import lax
from jax.experimental import pallas as pl
from jax.experimental.pallas import tpu as pltpu

F32 = jnp.float32
BF16 = jnp.bfloat16

D_MODEL = 1024
N_HEADS = 8
HEAD_W = 128
DA_HEAD_DIM = 64
CHUNK = 64
CONV_K = 4
N_BUCKETS = 32
MAX_DISTANCE = 128
NORM_EPS = 1e-6
SUBLN_EPS = 1e-5
HEAD_LN_EPS = 1e-5
NEG_INF = -1e30
LOG2E = 1.4426950408889634

COL_DA_Q, COL_DA_K, COL_DA_V, COL_DA_Z = 0, 1, 2, 3
COL_ML_Q, COL_ML_K, COL_ML_V, COL_ML_O, COL_ML_Z = 4, 5, 6, 7, 8
COL_GATE_A, COL_GATE_M = 9, 10
N_COL_BLOCKS = 11
N_MAIN_BLOCKS = 9

ATT_BLOCK = 256
ONES_ROWS = 16
ATT_PIPE = 2
ML_CHUNK = 128
HEAD_PIPE = 3
OUT_ROWS = 256
VMEM_LIMIT = 56 * 1024 * 1024


def _sigmoid(x):
    return 1.0 / (1.0 + jnp.exp2(x * -LOG2E))


def _dot(a, b):
    return jnp.dot(a, b, preferred_element_type=F32)


def _dot_nt(a, b):
    return lax.dot_general(a, b, (((1,), (1,)), ((), ())), preferred_element_type=F32)


def _dot_tn(a, b):
    return lax.dot_general(a, b, (((0,), (0,)), ((), ())), preferred_element_type=F32)


CONV_ROWS = 512
CONV_PAD = 8


def _inproj_kernel(x_ref, nw_ref, w_ref, wt_ref, wg_ref, cw_ref, cb_ref, o_ref, g_ref, xn_ref, y_ref, *, seq):
    j = pl.program_id(1)

    @pl.when(j == 0)
    def _():
        x = x_ref[...]
        ms = jnp.mean(x * x, axis=-1, keepdims=True)
        xn = (x * lax.rsqrt(ms + NORM_EPS) * nw_ref[...]).astype(BF16)
        xn_ref[...] = xn
        g_ref[...] = _dot(xn, wg_ref[...])
        y_ref[0:CONV_PAD, :] = jnp.zeros((CONV_PAD, D_MODEL), F32)

    is_conv = (j == COL_ML_Q) | (j == COL_ML_K)

    @pl.when(jnp.logical_not(is_conv) & (j < N_MAIN_BLOCKS))
    def _():
        o_ref[...] = _dot(xn_ref[...], w_ref[...]).astype(BF16)

    @pl.when(j >= N_MAIN_BLOCKS)
    def _():
        o_ref[...] = _dot(xn_ref[...], wt_ref[...]).astype(BF16)

    @pl.when(is_conv)
    def _():
        which = j - COL_ML_Q
        taps = cw_ref[which]
        bias = cb_ref[pl.ds(which, 1), :]

        for r in range(0, seq, CONV_ROWS):
            y_ref[CONV_PAD + r:CONV_PAD + r + CONV_ROWS, :] = _dot(xn_ref[r:r + CONV_ROWS, :], w_ref[...])
            win = y_ref[r:r + CONV_ROWS + CONV_PAD, :]
            acc = bias + taps[CONV_K - 1:CONV_K, :] * win[CONV_PAD:, :]
            for back in range(1, CONV_K):
                shifted = pltpu.roll(win, back, 0)[CONV_PAD:, :]
                acc = acc + taps[CONV_K - 1 - back:CONV_K - back, :] * shifted
            o_ref[r:r + CONV_ROWS, :] = (acc * _sigmoid(acc)).astype(BF16)


def _inproj(x2, norm_w, w_all, w_tail, w_gate, conv_w, conv_b, seq):
    m = x2.shape[0]
    n = N_COL_BLOCKS * D_MODEL
    return pl.pallas_call(
        functools.partial(_inproj_kernel, seq=seq),
        grid=(m // seq, N_COL_BLOCKS),
        in_specs=[
            pl.BlockSpec((seq, D_MODEL), lambda i, j: (i, 0)),
            pl.BlockSpec((1, D_MODEL), lambda i, j: (0, 0)),
            pl.BlockSpec((D_MODEL, D_MODEL), lambda i, j: (0, jnp.minimum(j, N_MAIN_BLOCKS - 1))),
            pl.BlockSpec((D_MODEL, D_MODEL), lambda i, j: (0, jnp.maximum(j - N_MAIN_BLOCKS, 0))),
            pl.BlockSpec((D_MODEL, 2 * HEAD_W), lambda i, j: (0, 0)),
            pl.BlockSpec((2, CONV_K, D_MODEL), lambda i, j: (0, 0, 0)),
            pl.BlockSpec((2, D_MODEL), lambda i, j: (0, 0)),
        ],
        out_specs=[
            pl.BlockSpec((seq, D_MODEL), lambda i, j: (i, j)),
            pl.BlockSpec((seq, 2 * HEAD_W), lambda i, j: (i, 0)),
        ],
        out_shape=[
            jax.ShapeDtypeStruct((m, n), BF16),
            jax.ShapeDtypeStruct((m, 2 * HEAD_W), F32),
        ],
        scratch_shapes=[
            pltpu.VMEM((seq, D_MODEL), BF16),
            pltpu.VMEM((seq + CONV_PAD, D_MODEL), F32),
        ],
        compiler_params=pltpu.CompilerParams(
            dimension_semantics=("arbitrary", "arbitrary"), vmem_limit_bytes=VMEM_LIMIT),
        name="inproj",
    )(x2, norm_w, w_all, w_tail, w_gate, conv_w, conv_b)


def _rel_bucket_np(rel):
    nb = N_BUCKETS // 2
    max_exact = nb // 2
    bucket = np.where(rel > 0, nb, 0)
    n = np.abs(rel)
    nf = np.maximum(n, 1).astype(np.float32)
    large = max_exact + (np.log(nf / np.float32(max_exact)) / np.float32(math.log(MAX_DISTANCE / max_exact))
                         * np.float32(nb - max_exact)).astype(np.int32)
    large = np.minimum(large, nb - 1)
    return bucket + np.where(n < max_exact, n, large)


def _bucket_tiles():
    dk = np.arange(ATT_BLOCK)[:, None]
    dq = np.arange(ATT_BLOCK)[None, :]
    diag = np.where(dk // CHUNK <= dq // CHUNK, _rel_bucket_np(dk - dq), -1)
    prev = _rel_bucket_np(dk - dq - ATT_BLOCK)
    return np.stack([diag, prev]).astype(np.int32)


FAR_BUCKET = int(_rel_bucket_np(np.array([-(ATT_BLOCK + 1)]))[0])


def _bias_kernel(rb_ref, bkt_ref, o_ref):
    h = pl.program_id(0)
    c_far = rb_ref[FAR_BUCKET, h]
    for t in range(2):
        bk = bkt_ref[t]
        acc = jnp.full((ATT_BLOCK, ATT_BLOCK), NEG_INF, F32)
        for n in range(N_BUCKETS):
            acc = jnp.where(bk == n, (rb_ref[n, h] - c_far) * LOG2E, acc)
        o_ref[0, t] = acc


def _bias_tiles(rel_bias):
    return pl.pallas_call(
        _bias_kernel,
        grid=(N_HEADS,),
        in_specs=[
            pl.BlockSpec(memory_space=pltpu.SMEM),
            pl.BlockSpec((2, ATT_BLOCK, ATT_BLOCK), lambda h: (0, 0, 0)),
        ],
        out_specs=pl.BlockSpec((1, 2, ATT_BLOCK, ATT_BLOCK), lambda h: (h, 0, 0, 0)),
        out_shape=jax.ShapeDtypeStruct((N_HEADS, 2, ATT_BLOCK, ATT_BLOCK), F32),
        name="bias_tiles",
    )(rel_bias, jnp.asarray(_bucket_tiles()))


def _attn_kernel(lam_ref, subln_ref, bias_ref, q_ref, k_ref, v_ref, z_ref, o_ref,
                 vt_ref, s_ref, p_ref, *, seq, lambda_init):
    tb = ATT_BLOCK
    lamv = lam_ref[...]
    lam = (jnp.exp(jnp.sum(lamv[0:1] * lamv[1:2], axis=-1, keepdims=True))
           - jnp.exp(jnp.sum(lamv[2:3] * lamv[3:4], axis=-1, keepdims=True)) + lambda_init)

    for i in range(seq // HEAD_W):
        sl = slice(i * HEAD_W, (i + 1) * HEAD_W)
        vt_ref[0:HEAD_W, sl] = v_ref[0, sl, :].astype(F32).T.astype(BF16)
    vt_ref[HEAD_W:, :] = jnp.ones((ONES_ROWS, seq), BF16)

    lane = lax.broadcasted_iota(jnp.int32, (tb, HEAD_W), 1)
    blocks = lambda j: slice(j * tb, (j + 1) * tb)
    n_q = seq // tb
    slot = lambda qi: qi % (ATT_PIPE + 1)
    q_both, col_max, normed, pv_acc = {}, {}, {}, {}

    def qk_task(qi, j):
        if qi not in q_both:
            q = (q_ref[0, blocks(qi), :].astype(F32) * (DA_HEAD_DIM ** -0.5 * LOG2E)).astype(BF16)
            q_both[qi] = jnp.concatenate([jnp.where(lane < DA_HEAD_DIM, q, jnp.zeros_like(q)),
                                          jnp.where(lane >= DA_HEAD_DIM, q, jnp.zeros_like(q))], axis=0)
        s = _dot_nt(k_ref[0, blocks(j), :], q_both[qi])
        if j >= qi - 1:
            bias = bias_ref[0, qi - j]
            s = s + jnp.concatenate([bias, bias], axis=1)
        s_ref[slot(qi), blocks(j), :] = s

    def finish_logits(qi):
        m = None
        for j in range(qi + 1):
            bm = jnp.max(s_ref[slot(qi), blocks(j), :].reshape(tb // 8, 8, 2 * tb), axis=0)
            m = bm if m is None else jnp.maximum(m, bm)
        col_max[qi] = jnp.max(m, axis=0, keepdims=True)

    def exp_task(qi, j):
        p_ref[blocks(j), :] = jnp.exp2((s_ref[slot(qi), blocks(j), :] - col_max[qi]).astype(BF16))

    def pv_task(qi, j):
        ba = _dot(vt_ref[:, blocks(j)], p_ref[blocks(j), :])
        pv_acc[qi] = ba if qi not in pv_acc else pv_acc[qi] + ba

    def epilogue(qi):
        total = pv_acc.pop(qi)
        o = None
        for p in range(2):
            cols = slice(p * tb, (p + 1) * tb)
            branch = total[0:HEAD_W, cols] * (1.0 / total[HEAD_W:HEAD_W + 1, cols])
            o = branch if o is None else o - lam * branch
        ms = jnp.mean(o * o, axis=0, keepdims=True)
        on = (o * lax.rsqrt(ms + SUBLN_EPS)).T
        z = z_ref[0, blocks(qi), :].astype(F32)
        y = on * subln_ref[...] * (1.0 - lambda_init) * (z * _sigmoid(z))
        o_ref[0, blocks(qi), :] = y.astype(o_ref.dtype)

    for r in range(n_q + ATT_PIPE):
        unit = r if r < n_q else None
        late = r - ATT_PIPE if r >= ATT_PIPE else None
        n_unit = unit + 1 if unit is not None else 0
        n_late = late + 1 if late is not None else 0
        for i in range(max(n_unit, n_late + 1 if n_late else 0)):
            if i < n_unit:
                qk_task(unit, i)
            if i < n_late:
                exp_task(late, i)
            if 1 <= i <= n_late:
                pv_task(late, i - 1)
        if unit is not None:
            finish_logits(unit)
        if late is not None:
            epilogue(late)


def _attention(proj3, bias_tiles, lam, subln_w, lambda_init):
    b, s, _ = proj3.shape

    def col(block):
        return pl.BlockSpec((1, s, HEAD_W), lambda bi, hi: (bi, 0, block * N_HEADS + hi))

    return pl.pallas_call(
        functools.partial(_attn_kernel, seq=s, lambda_init=lambda_init),
        grid=(b, N_HEADS),
        in_specs=[
            pl.BlockSpec((4, DA_HEAD_DIM), lambda bi, hi: (0, 0)),
            pl.BlockSpec((1, HEAD_W), lambda bi, hi: (0, 0)),
            pl.BlockSpec((1, 2, ATT_BLOCK, ATT_BLOCK), lambda bi, hi: (hi, 0, 0, 0)),
            col(COL_DA_Q), col(COL_DA_K), col(COL_DA_V), col(COL_DA_Z),
        ],
        out_specs=pl.BlockSpec((1, s, HEAD_W), lambda bi, hi: (bi, 0, hi)),
        out_shape=jax.ShapeDtypeStruct((b, s, D_MODEL), BF16),
        scratch_shapes=[
            pltpu.VMEM((HEAD_W + ONES_ROWS, s), BF16),
            pltpu.VMEM((ATT_PIPE + 1, s, 2 * ATT_BLOCK), F32),
            pltpu.VMEM((s, 2 * ATT_BLOCK), BF16),
        ],
        compiler_params=pltpu.CompilerParams(
            dimension_semantics=("arbitrary", "arbitrary"), vmem_limit_bytes=VMEM_LIMIT),
        name="diff_attention",
    )(lam, subln_w, bias_tiles, proj3, proj3, proj3, proj3)


_BCAST_PARTS = (3, 3, 2, 2)
_DECAY_PARTS = 2


_BCAST_W = len(_BCAST_PARTS) * HEAD_W


def _bcast_select_np():
    e = np.zeros((HEAD_W, N_HEADS * _BCAST_W), np.float32)
    g = 0
    for o, parts in enumerate(_BCAST_PARTS):
        for _ in range(parts):
            for h in range(N_HEADS):
                e[8 * g + h, h * _BCAST_W + o * HEAD_W:h * _BCAST_W + (o + 1) * HEAD_W] = 1.0
            g += 1
    return e


def _row_select_np():
    e = np.zeros((HEAD_W, N_HEADS * HEAD_W), np.float32)
    for g in range(_DECAY_PARTS):
        for h in range(N_HEADS):
            e[8 * g + h, h * HEAD_W:(h + 1) * HEAD_W] = 1.0
    return e


def _split_parts(x, n):
    parts = []
    r = x
    for i in range(n):
        p = r.astype(BF16).astype(F32)
        parts.append(p)
        if i + 1 < n:
            r = r - p
    return parts


def _pack_groups(arrays_and_parts, group):
    parts = []
    for x, n in arrays_and_parts:
        parts.extend(_split_parts(x, n))
    packed = jnp.zeros_like(parts[0])
    for gi, p in enumerate(parts):
        packed = jnp.where(group == gi, p, packed)
    return packed.astype(BF16)


def _mlstm_kernel(q_ref, k_ref, v_ref, g_ref, bif_ref, mhw_ref, ltri_ref, esel_ref, rsel_ref, y_ref,
                  bc_ref, wpt_ref, dcrow_ref, c_ref, m_ref, *, tblk):
    L = ML_CHUNK
    n_chunks = tblk // L

    @pl.when(pl.program_id(1) == 0)
    def _():
        c_ref[...] = jnp.zeros_like(c_ref)
        m_ref[...] = jnp.zeros_like(m_ref)

    row = lax.broadcasted_iota(jnp.int32, (tblk, HEAD_W), 0) % L
    group = lax.broadcasted_iota(jnp.int32, (tblk, HEAD_W), 1) // 8
    ig = g_ref[0, :, 0:HEAD_W] + bif_ref[0:1, :]
    fg = g_ref[0, :, HEAD_W:2 * HEAD_W] + bif_ref[1:2, :]
    lf = jnp.minimum(fg, 0.0) - jnp.log(1.0 + jnp.exp(-jnp.abs(fg)))
    ltri = ltri_ref[...]
    lf_parts = [p.astype(BF16) for p in _split_parts(lf, 3)]
    bcum = jnp.concatenate(
        [sum(_dot(ltri, p[c * L:(c + 1) * L]) for p in lf_parts) for c in range(n_chunks)], axis=0)
    wp = ig - bcum
    cmax = wp
    sh = 1
    while sh < L:
        cmax = jnp.where(row >= sh, jnp.maximum(cmax, pltpu.roll(cmax, sh, 0)), cmax)
        sh *= 2
    m_rows, b_last_rows, m_next_rows = [], [], []
    m_state = m_ref[...]
    for c in range(n_chunks):
        last = (c + 1) * L - 1
        b_last = bcum[last:last + 1, :]
        m_next = b_last + jnp.maximum(m_state, cmax[last:last + 1, :])
        dcrow_ref[c] = _dot(
            _pack_groups([(jnp.broadcast_to(jnp.exp(b_last + m_state - m_next), (8, HEAD_W)), _DECAY_PARTS)],
                         group[0:8]), rsel_ref[...])
        m_rows.append(jnp.broadcast_to(m_state, (L, HEAD_W)))
        b_last_rows.append(jnp.broadcast_to(b_last, (L, HEAD_W)))
        m_next_rows.append(jnp.broadcast_to(m_next, (L, HEAD_W)))
        m_state = m_next
    m_ref[...] = m_state
    m_prev = jnp.concatenate(m_rows, axis=0)
    inter = bcum + m_prev
    m_t = jnp.maximum(inter, bcum + cmax)
    w_inter = jnp.exp(inter - m_t)
    w_state = jnp.exp(jnp.concatenate(b_last_rows, axis=0) + wp - jnp.concatenate(m_next_rows, axis=0))
    k_scale = HEAD_W ** -0.5
    bc_ref[...] = _dot(_pack_groups(
        [((bcum - m_t) * LOG2E + math.log2(k_scale), 3), (m_t * -LOG2E, 3), (w_inter * k_scale, 2),
         (w_state, 2)], group), esel_ref[...])
    wp2 = wp * LOG2E
    for c in range(n_chunks):
        wpt_ref[c] = wp2[c * L:(c + 1) * L].T

    causal = (lax.broadcasted_iota(jnp.int32, (L, HEAD_W), 1)
              <= lax.broadcasted_iota(jnp.int32, (L, HEAD_W), 0))
    ones_blk = jnp.ones((L, HEAD_W), BF16)

    def chunk(c, _):
        rows = pl.ds(pl.multiple_of(c * L, L), L)
        wp_t = wpt_ref[c]
        dc_rows = dcrow_ref[c]
        heads = [dict(hs=slice(h * HEAD_W, (h + 1) * HEAD_W)) for h in range(N_HEADS)]

        def bcast(h, o):
            return bc_ref[rows, h * _BCAST_W + o * HEAD_W:h * _BCAST_W + (o + 1) * HEAD_W]

        def stage_scores(h):
            st = heads[h]
            st["q"], st["k"], st["v"] = (ref[0, rows, st["hs"]] for ref in (q_ref, k_ref, v_ref))
            st["s"] = _dot_nt(st["q"], st["k"])
            ws_b = bcast(h, 3)
            wv = jnp.concatenate([(ws_b * st["v"].astype(F32)).astype(BF16), ws_b.astype(BF16)], axis=1)
            st["upd"] = _dot_tn(st["k"], wv)

        def stage_readout(h):
            st = heads[h]
            w_intra = jnp.exp2(jnp.where(causal, bcast(h, 0) + wp_t[h:h + 1, :], NEG_INF))
            scores = (st["s"] * w_intra).astype(BF16)
            lhs = jnp.concatenate([scores, (st["q"].astype(F32) * bcast(h, 2)).astype(BF16)], axis=1)
            st["c_old"] = c_ref[h]
            rhs = jnp.concatenate(
                [jnp.concatenate([st["v"], ones_blk], axis=1), st["c_old"].astype(BF16)], axis=0)
            st["tot"] = _dot(lhs, rhs)

        def stage_finish(h):
            st = heads[h]
            tot = st["tot"]
            hh = tot[:, 0:HEAD_W] / jnp.maximum(jnp.abs(tot[:, HEAD_W:]), jnp.exp2(bcast(h, 1)))
            mu = jnp.mean(hh, axis=-1, keepdims=True)
            hc = hh - mu
            var = jnp.mean(hc * hc, axis=-1, keepdims=True)
            y_ref[0, rows, st["hs"]] = (
                hc * lax.rsqrt(var + HEAD_LN_EPS) * mhw_ref[:, st["hs"]]).astype(y_ref.dtype)
            dc = dc_rows[0:1, st["hs"]]
            c_ref[h] = jnp.concatenate([dc, dc], axis=1) * st["c_old"] + st["upd"]
            st.clear()

        for i in range(N_HEADS + 2 * HEAD_PIPE):
            if i < N_HEADS:
                stage_scores(i)
            if 0 <= i - HEAD_PIPE < N_HEADS:
                stage_readout(i - HEAD_PIPE)
            if 0 <= i - 2 * HEAD_PIPE < N_HEADS:
                stage_finish(i - 2 * HEAD_PIPE)
        return 0

    lax.fori_loop(0, n_chunks, chunk, 0)


def _mlstm(proj3, gates3, bif, mh_w, tblk):
    b, s, _ = proj3.shape
    n_chunks = tblk // ML_CHUNK

    def col(block):
        return pl.BlockSpec((1, tblk, D_MODEL), lambda bi, ti: (bi, ti, block))

    def full(shape):
        return pl.BlockSpec(shape, lambda bi, ti: (0,) * len(shape))

    ltri = jnp.asarray(np.tril(np.ones((ML_CHUNK, ML_CHUNK), np.float32)), BF16)
    esel = jnp.asarray(_bcast_select_np(), BF16)
    rsel = jnp.asarray(_row_select_np(), BF16)
    return pl.pallas_call(
        functools.partial(_mlstm_kernel, tblk=tblk),
        grid=(b, s // tblk),
        in_specs=[
            col(COL_ML_Q), col(COL_ML_K), col(COL_ML_V),
            pl.BlockSpec((1, tblk, 2 * HEAD_W), lambda bi, ti: (bi, ti, 0)),
            full((2, HEAD_W)), full((1, D_MODEL)),
            full((ML_CHUNK, ML_CHUNK)), full(esel.shape), full(rsel.shape),
        ],
        out_specs=pl.BlockSpec((1, tblk, D_MODEL), lambda bi, ti: (bi, ti, 0)),
        out_shape=jax.ShapeDtypeStruct((b, s, D_MODEL), BF16),
        scratch_shapes=[
            pltpu.VMEM((tblk, N_HEADS * _BCAST_W), F32),
            pltpu.VMEM((n_chunks, HEAD_W, ML_CHUNK), F32),
            pltpu.VMEM((n_chunks, 8, D_MODEL), F32),
            pltpu.VMEM((N_HEADS, HEAD_W, 2 * HEAD_W), F32),
            pltpu.VMEM((1, HEAD_W), F32),
        ],
        compiler_params=pltpu.CompilerParams(
            dimension_semantics=("arbitrary", "arbitrary"), vmem_limit_bytes=VMEM_LIMIT),
        name="mlstm",
    )(proj3, proj3, proj3, gates3, bif, mh_w, ltri, esel, rsel)


def _out_kernel(ya_ref, hn_ref, og_ref, zg_ref, ga_ref, gm_ref, x_ref, wpa_ref, wpm_ref, wout_ref, bg_ref,
                nf_ref, o_ref, *, final_norm):
    branches = {}

    def project_branches(r):
        rows = slice(r, r + OUT_ROWS)
        zg = zg_ref[rows, :].astype(F32)
        y_m = ((_sigmoid(og_ref[rows, :].astype(F32)) * hn_ref[rows, :].astype(F32))
               * (zg * _sigmoid(zg))).astype(BF16)
        branches[r] = (_dot(ya_ref[rows, :], wpa_ref[...]), _dot(y_m, wpm_ref[...]))

    def merge_and_project(r):
        rows = slice(r, r + OUT_ROWS)
        a, b = branches.pop(r)
        g_a = _sigmoid(ga_ref[rows, :].astype(F32) + bg_ref[0:1, :])
        g_m = _sigmoid(gm_ref[rows, :].astype(F32) + bg_ref[1:2, :])
        hres = x_ref[rows, :] + _dot((g_a * a + g_m * b).astype(BF16), wout_ref[...])
        if final_norm:
            ms = jnp.mean(hres * hres, axis=-1, keepdims=True)
            hres = hres * lax.rsqrt(ms + NORM_EPS) * nf_ref[...]
        o_ref[rows, :] = hres

    tm = o_ref.shape[0]
    project_branches(0)
    for r in range(0, tm, OUT_ROWS):
        if r + OUT_ROWS < tm:
            project_branches(r + OUT_ROWS)
        merge_and_project(r)


def _out_stage(y_a, h_n, proj, x2, w_pa, w_pm, w_out, b_gate, norm_final, tm, final_norm):
    m = x2.shape[0]

    def rows(block=0):
        return pl.BlockSpec((tm, D_MODEL), lambda i: (i, block))

    def full(shape):
        return pl.BlockSpec(shape, lambda i: (0,) * len(shape))

    return pl.pallas_call(
        functools.partial(_out_kernel, final_norm=final_norm),
        grid=(m // tm,),
        in_specs=[
            rows(), rows(), rows(COL_ML_O), rows(COL_ML_Z), rows(COL_GATE_A), rows(COL_GATE_M), rows(),
            full((D_MODEL, D_MODEL)), full((D_MODEL, D_MODEL)), full((D_MODEL, D_MODEL)),
            full((2, D_MODEL)), full((1, D_MODEL)),
        ],
        out_specs=rows(),
        out_shape=jax.ShapeDtypeStruct((m, D_MODEL), F32),
        compiler_params=pltpu.CompilerParams(
            dimension_semantics=("arbitrary",), vmem_limit_bytes=VMEM_LIMIT),
        name="out_stage",
    )(y_a, h_n, proj, proj, proj, proj, x2, w_pa, w_pm, w_out, b_gate, norm_final)


def _split_in_weight(w):
    n_wide = N_MAIN_BLOCKS * D_MODEL
    w_all = w.astype(BF16)
    w_tail = w_all[:, n_wide + 2 * N_HEADS:]
    w_i = w_all[:, n_wide:n_wide + N_HEADS]
    w_f = w_all[:, n_wide + N_HEADS:n_wide + 2 * N_HEADS]
    reps = HEAD_W // N_HEADS
    w_gate = jnp.concatenate([jnp.tile(w_i, (1, reps)), jnp.tile(w_f, (1, reps))], axis=1)
    return w_all, w_tail, w_gate


def kernel(x, norm_w, w_in, lam, subln_w, rel_bias, conv_w, conv_b, b_if, mh_w, b_gate, w_pa, w_pm, w_out,
           norm_final):
    b, s, d = x.shape
    depth = norm_w.shape[0]
    m = b * s
    tm_out = min(1024, m)
    tblk = min(512, s)
    bias_tiles = _bias_tiles(rel_bias)
    h2 = x.reshape(m, d)
    for layer in range(depth):
        lambda_init = 0.8 - 0.6 * math.exp(-0.3 * layer)
        w_all, w_tail, w_gate = _split_in_weight(w_in[layer])
        proj, gates = _inproj(h2, norm_w[layer][None, :], w_all, w_tail, w_gate, conv_w[layer], conv_b[layer], s)
        proj3 = proj.reshape(b, s, N_COL_BLOCKS * D_MODEL)
        y_a = _attention(proj3, bias_tiles, lam[layer], subln_w[layer][None, :], lambda_init)
        bif = jnp.tile(b_if[layer], (1, HEAD_W // N_HEADS))
        h_n = _mlstm(proj3, gates.reshape(b, s, 2 * HEAD_W), bif, mh_w[layer][None, :], tblk)
        h2 = _out_stage(y_a.reshape(m, d), h_n.reshape(m, d), proj, h2,
                        w_pa[layer].astype(BF16), w_pm[layer].astype(BF16), w_out[layer].astype(BF16),
                        b_gate[layer], norm_final[None, :], tm_out, final_norm=(layer == depth - 1))
    return h2.reshape(b, s, d)
```

```python
import functools
import math

import numpy as np
import jax
import jax.numpy as jnp
from jax import lax
from jax.experimental import pallas as pl
from jax.experimental.pallas import tpu as pltpu

F32 = jnp.float32
BF16 = jnp.bfloat16

D_MODEL = 1024
N_HEADS = 8
HEAD_W = 128
DA_HEAD_DIM = 64
CHUNK = 64
CONV_K = 4
N_BUCKETS = 32
MAX_DISTANCE = 128
NORM_EPS = 1e-6
SUBLN_EPS = 1e-5
HEAD_LN_EPS = 1e-5
NEG_INF = -1e30
LOG2E = 1.4426950408889634

COL_DA_Q, COL_DA_K, COL_DA_V, COL_DA_Z = 0, 1, 2, 3
COL_ML_Q, COL_ML_K, COL_ML_V, COL_ML_O, COL_ML_Z = 4, 5, 6, 7, 8
COL_GATE_A, COL_GATE_M = 9, 10
N_COL_BLOCKS = 11
N_MAIN_BLOCKS = 9

ATT_BLOCK = 256
ONES_ROWS = 16
ATT_PIPE = 2
ML_CHUNK = 128
HEAD_PIPE = 3
OUT_ROWS = 256
VMEM_LIMIT = 56 * 1024 * 1024


def _sigmoid(x):
    return 1.0 / (1.0 + jnp.exp2(x * -LOG2E))


def _dot(a, b):
    return jnp.dot(a, b, preferred_element_type=F32)


def _dot_nt(a, b):
    return lax.dot_general(a, b, (((1,), (1,)), ((), ())), preferred_element_type=F32)


def _dot_tn(a, b):
    return lax.dot_general(a, b, (((0,), (0,)), ((), ())), preferred_element_type=F32)


CONV_ROWS = 512
CONV_PAD = 8


def _inproj_kernel(x_ref, nw_ref, w_ref, wt_ref, wg_ref, cw_ref, cb_ref, o_ref, g_ref, xn_ref, y_ref, wb_ref,
                   *, seq):
    j = pl.program_id(1)

    @pl.when(j == 0)
    def _():
        x = x_ref[...]
        ms = jnp.mean(x * x, axis=-1, keepdims=True)
        xn = (x * lax.rsqrt(ms + NORM_EPS) * nw_ref[...]).astype(BF16)
        xn_ref[...] = xn
        g_ref[...] = _dot_nt(xn, wg_ref[...])

    is_conv = (j == COL_ML_Q) | (j == COL_ML_K)

    @pl.when(jnp.logical_not(is_conv) & (j < N_MAIN_BLOCKS))
    def _():
        o_ref[...] = _dot_nt(xn_ref[...], w_ref[...].astype(BF16)).astype(BF16)

    @pl.when(j >= N_MAIN_BLOCKS)
    def _():
        o_ref[...] = _dot_nt(xn_ref[...], wt_ref[...]).astype(BF16)

    @pl.when(is_conv)
    def _():
        which = j - COL_ML_Q
        taps = cw_ref[which]
        bias = cb_ref[pl.ds(which, 1), :]

        y_ref[0:CONV_PAD, :] = jnp.zeros((CONV_PAD, D_MODEL), F32)
        wb_ref[...] = w_ref[...].astype(BF16)
        for r in range(0, seq, CONV_ROWS):
            y_ref[CONV_PAD:, :] = _dot_nt(xn_ref[r:r + CONV_ROWS, :], wb_ref[...])
            win = y_ref[...]
            acc = bias + taps[CONV_K - 1:CONV_K, :] * win[CONV_PAD:, :]
            for back in range(1, CONV_K):
                shifted = pltpu.roll(win, back, 0)[CONV_PAD:, :]
                acc = acc + taps[CONV_K - 1 - back:CONV_K - back, :] * shifted
            o_ref[r:r + CONV_ROWS, :] = (acc * _sigmoid(acc)).astype(BF16)
            y_ref[0:CONV_PAD, :] = win[CONV_ROWS:, :]


def _inproj(x2, norm_w, w_t, layer, tail_t, gate_t, conv_w, conv_b, seq):
    m = x2.shape[0]
    n = N_COL_BLOCKS * D_MODEL
    return pl.pallas_call(
        functools.partial(_inproj_kernel, seq=seq),
        grid=(m // seq, N_COL_BLOCKS),
        in_specs=[
            pl.BlockSpec((seq, D_MODEL), lambda i, j: (i, 0)),
            pl.BlockSpec((1, D_MODEL), lambda i, j: (0, 0)),
            pl.BlockSpec((None, D_MODEL, D_MODEL), lambda i, j: (layer, jnp.minimum(j, N_MAIN_BLOCKS - 1), 0)),
            pl.BlockSpec((D_MODEL, D_MODEL), lambda i, j: (jnp.maximum(j - N_MAIN_BLOCKS, 0), 0)),
            pl.BlockSpec((2 * HEAD_W, D_MODEL), lambda i, j: (0, 0)),
            pl.BlockSpec((2, CONV_K, D_MODEL), lambda i, j: (0, 0, 0)),
            pl.BlockSpec((2, D_MODEL), lambda i, j: (0, 0)),
        ],
        out_specs=[
            pl.BlockSpec((seq, D_MODEL), lambda i, j: (i, j)),
            pl.BlockSpec((seq, 2 * HEAD_W), lambda i, j: (i, 0)),
        ],
        out_shape=[
            jax.ShapeDtypeStruct((m, n), BF16),
            jax.ShapeDtypeStruct((m, 2 * HEAD_W), F32),
        ],
        scratch_shapes=[
            pltpu.VMEM((seq, D_MODEL), BF16),
            pltpu.VMEM((CONV_ROWS + CONV_PAD, D_MODEL), F32),
            pltpu.VMEM((D_MODEL, D_MODEL), BF16),
        ],
        compiler_params=pltpu.CompilerParams(
            dimension_semantics=("arbitrary", "arbitrary"), vmem_limit_bytes=VMEM_LIMIT),
        name="inproj",
    )(x2, norm_w, w_t, tail_t, gate_t, conv_w, conv_b)


def _rel_bucket_np(rel):
    nb = N_BUCKETS // 2
    max_exact = nb // 2
    bucket = np.where(rel > 0, nb, 0)
    n = np.abs(rel)
    nf = np.maximum(n, 1).astype(np.float32)
    large = max_exact + (np.log(nf / np.float32(max_exact)) / np.float32(math.log(MAX_DISTANCE / max_exact))
                         * np.float32(nb - max_exact)).astype(np.int32)
    large = np.minimum(large, nb - 1)
    return bucket + np.where(n < max_exact, n, large)


def _bucket_tiles():
    dk = np.arange(ATT_BLOCK)[:, None]
    dq = np.arange(ATT_BLOCK)[None, :]
    diag = np.where(dk // CHUNK <= dq // CHUNK, _rel_bucket_np(dk - dq), -1)
    prev = _rel_bucket_np(dk - dq - ATT_BLOCK)
    return np.stack([diag, prev]).astype(np.int32)


FAR_BUCKET = int(_rel_bucket_np(np.array([-(ATT_BLOCK + 1)]))[0])


def _bias_kernel(rb_ref, bkt_ref, o_ref):
    h = pl.program_id(0)
    c_far = rb_ref[FAR_BUCKET, h]
    for t in range(2):
        bk = bkt_ref[t]
        acc = jnp.full((ATT_BLOCK, ATT_BLOCK), NEG_INF, F32)
        for n in range(N_BUCKETS):
            acc = jnp.where(bk == n, (rb_ref[n, h] - c_far) * LOG2E, acc)
        o_ref[0, t] = acc


def _bias_tiles(rel_bias):
    return pl.pallas_call(
        _bias_kernel,
        grid=(N_HEADS,),
        in_specs=[
            pl.BlockSpec(memory_space=pltpu.SMEM),
            pl.BlockSpec((2, ATT_BLOCK, ATT_BLOCK), lambda h: (0, 0, 0)),
        ],
        out_specs=pl.BlockSpec((1, 2, ATT_BLOCK, ATT_BLOCK), lambda h: (h, 0, 0, 0)),
        out_shape=jax.ShapeDtypeStruct((N_HEADS, 2, ATT_BLOCK, ATT_BLOCK), F32),
        name="bias_tiles",
    )(rel_bias, jnp.asarray(_bucket_tiles()))


def _attn_kernel(lam_ref, subln_ref, bias_ref, q_ref, k_ref, v_ref, z_ref, o_ref,
                 vt_ref, s_ref, p_ref, *, seq, lambda_init):
    tb = ATT_BLOCK
    lamv = lam_ref[...]
    lam = (jnp.exp(jnp.sum(lamv[0:1] * lamv[1:2], axis=-1, keepdims=True))
           - jnp.exp(jnp.sum(lamv[2:3] * lamv[3:4], axis=-1, keepdims=True)) + lambda_init)

    for i in range(seq // HEAD_W):
        sl = slice(i * HEAD_W, (i + 1) * HEAD_W)
        vt_ref[0:HEAD_W, sl] = v_ref[0, sl, :].astype(F32).T.astype(BF16)
    vt_ref[HEAD_W:, :] = jnp.ones((ONES_ROWS, seq), BF16)

    lane = lax.broadcasted_iota(jnp.int32, (tb, HEAD_W), 1)
    blocks = lambda j: slice(j * tb, (j + 1) * tb)
    n_q = seq // tb
    slot = lambda qi: qi % (ATT_PIPE + 1)
    q_both, col_max, normed, pv_acc = {}, {}, {}, {}

    def qk_task(qi, j):
        if qi not in q_both:
            q = (q_ref[0, blocks(qi), :].astype(F32) * (DA_HEAD_DIM ** -0.5 * LOG2E)).astype(BF16)
            q_both[qi] = jnp.concatenate([jnp.where(lane < DA_HEAD_DIM, q, jnp.zeros_like(q)),
                                          jnp.where(lane >= DA_HEAD_DIM, q, jnp.zeros_like(q))], axis=0)
        s = _dot_nt(k_ref[0, blocks(j), :], q_both[qi])
        if j >= qi - 1:
            bias = bias_ref[0, qi - j]
            s = s + jnp.concatenate([bias, bias], axis=1)
        s_ref[slot(qi), blocks(j), :] = s

    def finish_logits(qi):
        m = None
        for j in range(qi + 1):
            bm = jnp.max(s_ref[slot(qi), blocks(j), :].reshape(tb // 8, 8, 2 * tb), axis=0)
            m = bm if m is None else jnp.maximum(m, bm)
        col_max[qi] = jnp.max(m, axis=0, keepdims=True)

    def exp_task(qi, j):
        p_ref[blocks(j), :] = jnp.exp2((s_ref[slot(qi), blocks(j), :] - col_max[qi]).astype(BF16))

    def pv_task(qi, j):
        ba = _dot(vt_ref[:, blocks(j)], p_ref[blocks(j), :])
        pv_acc[qi] = ba if qi not in pv_acc else pv_acc[qi] + ba

    def epilogue(qi):
        total = pv_acc.pop(qi)
        o = None
        for p in range(2):
            cols = slice(p * tb, (p + 1) * tb)
            branch = total[0:HEAD_W, cols] * (1.0 / total[HEAD_W:HEAD_W + 1, cols])
            o = branch if o is None else o - lam * branch
        ms = jnp.mean(o * o, axis=0, keepdims=True)
        on = (o * lax.rsqrt(ms + SUBLN_EPS)).T
        z = z_ref[0, blocks(qi), :].astype(F32)
        y = on * subln_ref[...] * (1.0 - lambda_init) * (z * _sigmoid(z))
        o_ref[0, blocks(qi), :] = y.astype(o_ref.dtype)

    for r in range(n_q + ATT_PIPE):
        unit = r if r < n_q else None
        late = r - ATT_PIPE if r >= ATT_PIPE else None
        n_unit = unit + 1 if unit is not None else 0
        n_late = late + 1 if late is not None else 0
        for i in range(max(n_unit, n_late + 1 if n_late else 0)):
            if i < n_unit:
                qk_task(unit, i)
            if i < n_late:
                exp_task(late, i)
            if 1 <= i <= n_late:
                pv_task(late, i - 1)
        if unit is not None:
            finish_logits(unit)
        if late is not None:
            epilogue(late)


def _attention(proj3, bias_tiles, lam, subln_w, lambda_init):
    b, s, _ = proj3.shape

    def col(block):
        return pl.BlockSpec((1, s, HEAD_W), lambda bi, hi: (bi, 0, block * N_HEADS + hi))

    return pl.pallas_call(
        functools.partial(_attn_kernel, seq=s, lambda_init=lambda_init),
        grid=(b, N_HEADS),
        in_specs=[
            pl.BlockSpec((4, DA_HEAD_DIM), lambda bi, hi: (0, 0)),
            pl.BlockSpec((1, HEAD_W), lambda bi, hi: (0, 0)),
            pl.BlockSpec((1, 2, ATT_BLOCK, ATT_BLOCK), lambda bi, hi: (hi, 0, 0, 0)),
            col(COL_DA_Q), col(COL_DA_K), col(COL_DA_V), col(COL_DA_Z),
        ],
        out_specs=pl.BlockSpec((1, s, HEAD_W), lambda bi, hi: (bi, 0, hi)),
        out_shape=jax.ShapeDtypeStruct((b, s, D_MODEL), BF16),
        scratch_shapes=[
            pltpu.VMEM((HEAD_W + ONES_ROWS, s), BF16),
            pltpu.VMEM((ATT_PIPE + 1, s, 2 * ATT_BLOCK), F32),
            pltpu.VMEM((s, 2 * ATT_BLOCK), BF16),
        ],
        compiler_params=pltpu.CompilerParams(
            dimension_semantics=("arbitrary", "arbitrary"), vmem_limit_bytes=VMEM_LIMIT),
        name="diff_attention",
    )(lam, subln_w, bias_tiles, proj3, proj3, proj3, proj3)


_BCAST_PARTS = (3, 3, 2, 2)
_DECAY_PARTS = 2


_BCAST_W = len(_BCAST_PARTS) * HEAD_W


def _bcast_select_np():
    e = np.zeros((HEAD_W, N_HEADS * _BCAST_W), np.float32)
    g = 0
    for o, parts in enumerate(_BCAST_PARTS):
        for _ in range(parts):
            for h in range(N_HEADS):
                e[8 * g + h, h * _BCAST_W + o * HEAD_W:h * _BCAST_W + (o + 1) * HEAD_W] = 1.0
            g += 1
    return e


def _row_select_np():
    e = np.zeros((HEAD_W, N_HEADS * HEAD_W), np.float32)
    for g in range(_DECAY_PARTS):
        for h in range(N_HEADS):
            e[8 * g + h, h * HEAD_W:(h + 1) * HEAD_W] = 1.0
    return e


def _split_parts(x, n):
    parts = []
    r = x
    for i in range(n):
        p = r.astype(BF16).astype(F32)
        parts.append(p)
        if i + 1 < n:
            r = r - p
    return parts


def _pack_groups(arrays_and_parts, group):
    parts = []
    for x, n in arrays_and_parts:
        parts.extend(_split_parts(x, n))
    packed = jnp.zeros_like(parts[0])
    for gi, p in enumerate(parts):
        packed = jnp.where(group == gi, p, packed)
    return packed.astype(BF16)


def _mlstm_kernel(q_ref, k_ref, v_ref, g_ref, bif_ref, mhw_ref, ltri_ref, esel_ref, rsel_ref, y_ref,
                  bc_ref, wpt_ref, dcrow_ref, c_ref, m_ref, *, tblk):
    L = ML_CHUNK
    n_chunks = tblk // L

    @pl.when(pl.program_id(1) == 0)
    def _():
        c_ref[...] = jnp.zeros_like(c_ref)
        m_ref[...] = jnp.zeros_like(m_ref)

    row = lax.broadcasted_iota(jnp.int32, (tblk, HEAD_W), 0) % L
    group = lax.broadcasted_iota(jnp.int32, (tblk, HEAD_W), 1) // 8
    ig = g_ref[0, :, 0:HEAD_W] + bif_ref[0:1, :]
    fg = g_ref[0, :, HEAD_W:2 * HEAD_W] + bif_ref[1:2, :]
    lf = jnp.minimum(fg, 0.0) - jnp.log(1.0 + jnp.exp(-jnp.abs(fg)))
    ltri = ltri_ref[...]
    lf_parts = [p.astype(BF16) for p in _split_parts(lf, 3)]
    bcum = jnp.concatenate(
        [sum(_dot(ltri, p[c * L:(c + 1) * L]) for p in lf_parts) for c in range(n_chunks)], axis=0)
    wp = ig - bcum
    cmax = wp
    sh = 1
    while sh < L:
        cmax = jnp.where(row >= sh, jnp.maximum(cmax, pltpu.roll(cmax, sh, 0)), cmax)
        sh *= 2
    m_rows, b_last_rows, m_next_rows = [], [], []
    m_state = m_ref[...]
    for c in range(n_chunks):
        last = (c + 1) * L - 1
        b_last = bcum[last:last + 1, :]
        m_next = b_last + jnp.maximum(m_state, cmax[last:last + 1, :])
        dcrow_ref[c] = _dot(
            _pack_groups([(jnp.broadcast_to(jnp.exp(b_last + m_state - m_next), (8, HEAD_W)), _DECAY_PARTS)],
                         group[0:8]), rsel_ref[...])
        m_rows.append(jnp.broadcast_to(m_state, (L, HEAD_W)))
        b_last_rows.append(jnp.broadcast_to(b_last, (L, HEAD_W)))
        m_next_rows.append(jnp.broadcast_to(m_next, (L, HEAD_W)))
        m_state = m_next
    m_ref[...] = m_state
    m_prev = jnp.concatenate(m_rows, axis=0)
    inter = bcum + m_prev
    m_t = jnp.maximum(inter, bcum + cmax)
    w_inter = jnp.exp(inter - m_t)
    w_state = jnp.exp(jnp.concatenate(b_last_rows, axis=0) + wp - jnp.concatenate(m_next_rows, axis=0))
    k_scale = HEAD_W ** -0.5
    bc_ref[...] = _dot(_pack_groups(
        [((bcum - m_t) * LOG2E + math.log2(k_scale), 3), (m_t * -LOG2E, 3), (w_inter * k_scale, 2),
         (w_state, 2)], group), esel_ref[...])
    wp2 = wp * LOG2E
    for c in range(n_chunks):
        wpt_ref[c] = wp2[c * L:(c + 1) * L].T

    causal = (lax.broadcasted_iota(jnp.int32, (L, HEAD_W), 1)
              <= lax.broadcasted_iota(jnp.int32, (L, HEAD_W), 0))
    ones_blk = jnp.ones((L, HEAD_W), BF16)

    def chunk(c, _):
        rows = pl.ds(pl.multiple_of(c * L, L), L)
        wp_t = wpt_ref[c]
        dc_rows = dcrow_ref[c]
        heads = [dict(hs=slice(h * HEAD_W, (h + 1) * HEAD_W)) for h in range(N_HEADS)]

        def bcast(h, o):
            return bc_ref[rows, h * _BCAST_W + o * HEAD_W:h * _BCAST_W + (o + 1) * HEAD_W]

        def stage_scores(h):
            st = heads[h]
            st["q"], st["k"], st["v"] = (ref[0, rows, st["hs"]] for ref in (q_ref, k_ref, v_ref))
            st["s"] = _dot_nt(st["q"], st["k"])
            ws_b = bcast(h, 3)
            wv = jnp.concatenate([(ws_b * st["v"].astype(F32)).astype(BF16), ws_b.astype(BF16)], axis=1)
            st["upd"] = _dot_tn(st["k"], wv)

        def stage_readout(h):
            st = heads[h]
            w_intra = jnp.exp2(jnp.where(causal, bcast(h, 0) + wp_t[h:h + 1, :], NEG_INF))
            scores = (st["s"] * w_intra).astype(BF16)
            lhs = jnp.concatenate([scores, (st["q"].astype(F32) * bcast(h, 2)).astype(BF16)], axis=1)
            st["c_old"] = c_ref[h]
            rhs = jnp.concatenate(
                [jnp.concatenate([st["v"], ones_blk], axis=1), st["c_old"].astype(BF16)], axis=0)
            st["tot"] = _dot(lhs, rhs)

        def stage_finish(h):
            st = heads[h]
            tot = st["tot"]
            hh = tot[:, 0:HEAD_W] / jnp.maximum(jnp.abs(tot[:, HEAD_W:]), jnp.exp2(bcast(h, 1)))
            mu = jnp.mean(hh, axis=-1, keepdims=True)
            hc = hh - mu
            var = jnp.mean(hc * hc, axis=-1, keepdims=True)
            y_ref[0, rows, st["hs"]] = (
                hc * lax.rsqrt(var + HEAD_LN_EPS) * mhw_ref[:, st["hs"]]).astype(y_ref.dtype)
            dc = dc_rows[0:1, st["hs"]]
            c_ref[h] = jnp.concatenate([dc, dc], axis=1) * st["c_old"] + st["upd"]
            st.clear()

        for i in range(N_HEADS + 2 * HEAD_PIPE):
            if i < N_HEADS:
                stage_scores(i)
            if 0 <= i - HEAD_PIPE < N_HEADS:
                stage_readout(i - HEAD_PIPE)
            if 0 <= i - 2 * HEAD_PIPE < N_HEADS:
                stage_finish(i - 2 * HEAD_PIPE)
        return 0

    lax.fori_loop(0, n_chunks, chunk, 0)


def _mlstm(proj3, gates3, bif, mh_w, tblk):
    b, s, _ = proj3.shape
    n_chunks = tblk // ML_CHUNK

    def col(block):
        return pl.BlockSpec((1, tblk, D_MODEL), lambda bi, ti: (bi, ti, block))

    def full(shape):
        return pl.BlockSpec(shape, lambda bi, ti: (0,) * len(shape))

    ltri = jnp.asarray(np.tril(np.ones((ML_CHUNK, ML_CHUNK), np.float32)), BF16)
    esel = jnp.asarray(_bcast_select_np(), BF16)
    rsel = jnp.asarray(_row_select_np(), BF16)
    return pl.pallas_call(
        functools.partial(_mlstm_kernel, tblk=tblk),
        grid=(b, s // tblk),
        in_specs=[
            col(COL_ML_Q), col(COL_ML_K), col(COL_ML_V),
            pl.BlockSpec((1, tblk, 2 * HEAD_W), lambda bi, ti: (bi, ti, 0)),
            full((2, HEAD_W)), full((1, D_MODEL)),
            full((ML_CHUNK, ML_CHUNK)), full(esel.shape), full(rsel.shape),
        ],
        out_specs=pl.BlockSpec((1, tblk, D_MODEL), lambda bi, ti: (bi, ti, 0)),
        out_shape=jax.ShapeDtypeStruct((b, s, D_MODEL), BF16),
        scratch_shapes=[
            pltpu.VMEM((tblk, N_HEADS * _BCAST_W), F32),
            pltpu.VMEM((n_chunks, HEAD_W, ML_CHUNK), F32),
            pltpu.VMEM((n_chunks, 8, D_MODEL), F32),
            pltpu.VMEM((N_HEADS, HEAD_W, 2 * HEAD_W), F32),
            pltpu.VMEM((1, HEAD_W), F32),
        ],
        compiler_params=pltpu.CompilerParams(
            dimension_semantics=("arbitrary", "arbitrary"), vmem_limit_bytes=VMEM_LIMIT),
        name="mlstm",
    )(proj3, proj3, proj3, gates3, bif, mh_w, ltri, esel, rsel)


def _out_kernel(ya_ref, hn_ref, og_ref, zg_ref, ga_ref, gm_ref, x_ref, wpa_ref, wpm_ref, wout_ref, bg_ref,
                nf_ref, o_ref, *, final_norm):
    branches = {}

    def project_branches(r):
        rows = slice(r, r + OUT_ROWS)
        zg = zg_ref[rows, :].astype(F32)
        y_m = ((_sigmoid(og_ref[rows, :].astype(F32)) * hn_ref[rows, :].astype(F32))
               * (zg * _sigmoid(zg))).astype(BF16)
        branches[r] = (_dot(ya_ref[rows, :], wpa_ref[...]), _dot(y_m, wpm_ref[...]))

    def merge_and_project(r):
        rows = slice(r, r + OUT_ROWS)
        a, b = branches.pop(r)
        g_a = _sigmoid(ga_ref[rows, :].astype(F32) + bg_ref[0:1, :])
        g_m = _sigmoid(gm_ref[rows, :].astype(F32) + bg_ref[1:2, :])
        hres = x_ref[rows, :] + _dot((g_a * a + g_m * b).astype(BF16), wout_ref[...])
        if final_norm:
            ms = jnp.mean(hres * hres, axis=-1, keepdims=True)
            hres = hres * lax.rsqrt(ms + NORM_EPS) * nf_ref[...]
        o_ref[rows, :] = hres

    tm = o_ref.shape[0]
    project_branches(0)
    for r in range(0, tm, OUT_ROWS):
        if r + OUT_ROWS < tm:
            project_branches(r + OUT_ROWS)
        merge_and_project(r)


def _out_stage(y_a, h_n, proj, x2, w_pa, w_pm, w_out, b_gate, norm_final, tm, final_norm):
    m = x2.shape[0]

    def rows(block=0):
        return pl.BlockSpec((tm, D_MODEL), lambda i: (i, block))

    def full(shape):
        return pl.BlockSpec(shape, lambda i: (0,) * len(shape))

    return pl.pallas_call(
        functools.partial(_out_kernel, final_norm=final_norm),
        grid=(m // tm,),
        in_specs=[
            rows(), rows(), rows(COL_ML_O), rows(COL_ML_Z), rows(COL_GATE_A), rows(COL_GATE_M), rows(),
            full((D_MODEL, D_MODEL)), full((D_MODEL, D_MODEL)), full((D_MODEL, D_MODEL)),
            full((2, D_MODEL)), full((1, D_MODEL)),
        ],
        out_specs=rows(),
        out_shape=jax.ShapeDtypeStruct((m, D_MODEL), F32),
        compiler_params=pltpu.CompilerParams(
            dimension_semantics=("arbitrary",), vmem_limit_bytes=VMEM_LIMIT),
        name="out_stage",
    )(y_a, h_n, proj, proj, proj, proj, x2, w_pa, w_pm, w_out, b_gate, norm_final)


def _transposed_in_weight(w_in, layer):
    w_t = jnp.swapaxes(w_in, 1, 2)
    n_wide = N_MAIN_BLOCKS * D_MODEL
    tail_t = w_t[layer, n_wide + 2 * N_HEADS:, :].astype(BF16)
    reps = HEAD_W // N_HEADS
    gate_t = jnp.concatenate([jnp.tile(w_t[layer, n_wide:n_wide + N_HEADS, :], (reps, 1)),
                              jnp.tile(w_t[layer, n_wide + N_HEADS:n_wide + 2 * N_HEADS, :], (reps, 1))],
                             axis=0).astype(BF16)
    return w_t, tail_t, gate_t


def kernel(x, norm_w, w_in, lam, subln_w, rel_bias, conv_w, conv_b, b_if, mh_w, b_gate, w_pa, w_pm, w_out,
           norm_final):
    b, s, d = x.shape
    depth = norm_w.shape[0]
    m = b * s
    tm_out = min(1024, m)
    tblk = min(512, s)
    bias_tiles = _bias_tiles(rel_bias)
    h2 = x.reshape(m, d)
    for layer in range(depth):
        lambda_init = 0.8 - 0.6 * math.exp(-0.3 * layer)
        w_t, tail_t, gate_t = _transposed_in_weight(w_in, layer)
        proj, gates = _inproj(h2, norm_w[layer][None, :], w_t, layer, tail_t, gate_t,
                              conv_w[layer], conv_b[layer], s)
        proj3 = proj.reshape(b, s, N_COL_BLOCKS * D_MODEL)
        y_a = _attention(proj3, bias_tiles, lam[layer], subln_w[layer][None, :], lambda_init)
        bif = jnp.tile(b_if[layer], (1, HEAD_W // N_HEADS))
        h_n = _mlstm(proj3, gates.reshape(b, s, 2 * HEAD_W), bif, mh_w[layer][None, :], tblk)
        h2 = _out_stage(y_a.reshape(m, d), h_n.reshape(m, d), proj, h2,
                        w_pa[layer].astype(BF16), w_pm[layer].astype(BF16), w_out[layer].astype(BF16),
                        b_gate[layer], norm_final[None, :], tm_out, final_norm=(layer == depth - 1))
    return h2.reshape(b, s, d)
```

```python
import functools
import math

import numpy as np
import jax
import jax.numpy as jnp
from jax import lax
from jax.experimental import pallas as pl
from jax.experimental.pallas import tpu as pltpu

F32 = jnp.float32
BF16 = jnp.bfloat16

D_MODEL = 1024
N_HEADS = 8
HEAD_W = 128
DA_HEAD_DIM = 64
CHUNK = 64
CONV_K = 4
N_BUCKETS = 32
MAX_DISTANCE = 128
NORM_EPS = 1e-6
SUBLN_EPS = 1e-5
HEAD_LN_EPS = 1e-5
NEG_INF = -1e30
LOG2E = 1.4426950408889634

COL_DA_Q, COL_DA_K, COL_DA_V, COL_DA_Z = 0, 1, 2, 3
COL_ML_Q, COL_ML_K, COL_ML_V, COL_ML_O, COL_ML_Z = 4, 5, 6, 7, 8
COL_GATE_A, COL_GATE_M = 9, 10
N_COL_BLOCKS = 11
N_MAIN_BLOCKS = 9

ATT_BLOCK = 256
ONES_ROWS = 16
ATT_PIPE = 2
ATT_HEADS = 2
ML_CHUNK = 128
HEAD_PIPE = 3
OUT_ROWS = 256
VMEM_LIMIT = 56 * 1024 * 1024


def _sigmoid(x):
    return 1.0 / (1.0 + jnp.exp2(x * -LOG2E))


def _dot(a, b):
    return jnp.dot(a, b, preferred_element_type=F32)


def _dot_nt(a, b):
    return lax.dot_general(a, b, (((1,), (1,)), ((), ())), preferred_element_type=F32)


def _dot_tn(a, b):
    return lax.dot_general(a, b, (((0,), (0,)), ((), ())), preferred_element_type=F32)


CONV_ROWS = 512
CONV_PAD = 8


def _inproj_kernel(x_ref, nw_ref, w_ref, wt_ref, wg_ref, cw_ref, cb_ref, o_ref, g_ref, xn_ref, y_ref, wb_ref,
                   *, seq):
    j = pl.program_id(1)

    @pl.when(j == 0)
    def _():
        x = x_ref[...]
        ms = jnp.mean(x * x, axis=-1, keepdims=True)
        xn = (x * lax.rsqrt(ms + NORM_EPS) * nw_ref[...]).astype(BF16)
        xn_ref[...] = xn
        g_ref[...] = _dot_nt(xn, wg_ref[...])

    is_conv = (j == COL_ML_Q) | (j == COL_ML_K)

    @pl.when(jnp.logical_not(is_conv) & (j < N_MAIN_BLOCKS))
    def _():
        o_ref[...] = _dot_nt(xn_ref[...], w_ref[...].astype(BF16)).astype(BF16)

    @pl.when(j >= N_MAIN_BLOCKS)
    def _():
        o_ref[...] = _dot_nt(xn_ref[...], wt_ref[...]).astype(BF16)

    @pl.when(is_conv)
    def _():
        which = j - COL_ML_Q
        taps = cw_ref[which]
        bias = cb_ref[pl.ds(which, 1), :]

        y_ref[0:CONV_PAD, :] = jnp.zeros((CONV_PAD, D_MODEL), F32)
        wb_ref[...] = w_ref[...].astype(BF16)
        for r in range(0, seq, CONV_ROWS):
            y_ref[CONV_PAD:, :] = _dot_nt(xn_ref[r:r + CONV_ROWS, :], wb_ref[...])
            win = y_ref[...]
            acc = bias + taps[CONV_K - 1:CONV_K, :] * win[CONV_PAD:, :]
            for back in range(1, CONV_K):
                shifted = pltpu.roll(win, back, 0)[CONV_PAD:, :]
                acc = acc + taps[CONV_K - 1 - back:CONV_K - back, :] * shifted
            o_ref[r:r + CONV_ROWS, :] = (acc * _sigmoid(acc)).astype(BF16)
            y_ref[0:CONV_PAD, :] = win[CONV_ROWS:, :]


def _inproj(x2, norm_w, w_t, layer, tail_t, gate_t, conv_w, conv_b, seq):
    m = x2.shape[0]
    n = N_COL_BLOCKS * D_MODEL
    return pl.pallas_call(
        functools.partial(_inproj_kernel, seq=seq),
        grid=(m // seq, N_COL_BLOCKS),
        in_specs=[
            pl.BlockSpec((seq, D_MODEL), lambda i, j: (i, 0)),
            pl.BlockSpec((1, D_MODEL), lambda i, j: (0, 0)),
            pl.BlockSpec((None, D_MODEL, D_MODEL), lambda i, j: (layer, jnp.minimum(j, N_MAIN_BLOCKS - 1), 0)),
            pl.BlockSpec((D_MODEL, D_MODEL), lambda i, j: (jnp.maximum(j - N_MAIN_BLOCKS, 0), 0)),
            pl.BlockSpec((2 * HEAD_W, D_MODEL), lambda i, j: (0, 0)),
            pl.BlockSpec((2, CONV_K, D_MODEL), lambda i, j: (0, 0, 0)),
            pl.BlockSpec((2, D_MODEL), lambda i, j: (0, 0)),
        ],
        out_specs=[
            pl.BlockSpec((seq, D_MODEL), lambda i, j: (i, j)),
            pl.BlockSpec((seq, 2 * HEAD_W), lambda i, j: (i, 0)),
        ],
        out_shape=[
            jax.ShapeDtypeStruct((m, n), BF16),
            jax.ShapeDtypeStruct((m, 2 * HEAD_W), F32),
        ],
        scratch_shapes=[
            pltpu.VMEM((seq, D_MODEL), BF16),
            pltpu.VMEM((CONV_ROWS + CONV_PAD, D_MODEL), F32),
            pltpu.VMEM((D_MODEL, D_MODEL), BF16),
        ],
        compiler_params=pltpu.CompilerParams(
            dimension_semantics=("arbitrary", "arbitrary"), vmem_limit_bytes=VMEM_LIMIT),
        name="inproj",
    )(x2, norm_w, w_t, tail_t, gate_t, conv_w, conv_b)


def _rel_bucket_np(rel):
    nb = N_BUCKETS // 2
    max_exact = nb // 2
    bucket = np.where(rel > 0, nb, 0)
    n = np.abs(rel)
    nf = np.maximum(n, 1).astype(np.float32)
    large = max_exact + (np.log(nf / np.float32(max_exact)) / np.float32(math.log(MAX_DISTANCE / max_exact))
                         * np.float32(nb - max_exact)).astype(np.int32)
    large = np.minimum(large, nb - 1)
    return bucket + np.where(n < max_exact, n, large)


def _bucket_tiles():
    dk = np.arange(ATT_BLOCK)[:, None]
    dq = np.arange(ATT_BLOCK)[None, :]
    diag = np.where(dk // CHUNK <= dq // CHUNK, _rel_bucket_np(dk - dq), -1)
    prev = _rel_bucket_np(dk - dq - ATT_BLOCK)
    return np.stack([diag, prev]).astype(np.int32)


FAR_BUCKET = int(_rel_bucket_np(np.array([-(ATT_BLOCK + 1)]))[0])


def _bias_kernel(rb_ref, bkt_ref, o_ref):
    h = pl.program_id(0)
    c_far = rb_ref[FAR_BUCKET, h]
    for t in range(2):
        bk = bkt_ref[t]
        acc = jnp.full((ATT_BLOCK, ATT_BLOCK), NEG_INF, F32)
        for n in range(N_BUCKETS):
            acc = jnp.where(bk == n, (rb_ref[n, h] - c_far) * LOG2E, acc)
        o_ref[0, t] = acc


def _bias_tiles(rel_bias):
    return pl.pallas_call(
        _bias_kernel,
        grid=(N_HEADS,),
        in_specs=[
            pl.BlockSpec(memory_space=pltpu.SMEM),
            pl.BlockSpec((2, ATT_BLOCK, ATT_BLOCK), lambda h: (0, 0, 0)),
        ],
        out_specs=pl.BlockSpec((1, 2, ATT_BLOCK, ATT_BLOCK), lambda h: (h, 0, 0, 0)),
        out_shape=jax.ShapeDtypeStruct((N_HEADS, 2, ATT_BLOCK, ATT_BLOCK), F32),
        name="bias_tiles",
    )(rel_bias, jnp.asarray(_bucket_tiles()))


def _attn_kernel(lam_ref, subln_ref, bias_ref, q_ref, k_ref, v_ref, z_ref, o_ref,
                 vt_ref, s_ref, *, seq, lambda_init):
    tb = ATT_BLOCK
    lamv = lam_ref[...]
    lam = (jnp.exp(jnp.sum(lamv[0:1] * lamv[1:2], axis=-1, keepdims=True))
           - jnp.exp(jnp.sum(lamv[2:3] * lamv[3:4], axis=-1, keepdims=True)) + lambda_init)
    head_cols = lambda h: slice(h * HEAD_W, (h + 1) * HEAD_W)

    for h in range(ATT_HEADS):
        for i in range(seq // HEAD_W):
            sl = slice(i * HEAD_W, (i + 1) * HEAD_W)
            vt_ref[h, 0:HEAD_W, sl] = v_ref[0, sl, head_cols(h)].astype(F32).T.astype(BF16)
        vt_ref[h, HEAD_W:, :] = jnp.ones((ONES_ROWS, seq), BF16)

    lane = lax.broadcasted_iota(jnp.int32, (tb, HEAD_W), 1)
    blocks = lambda j: slice(j * tb, (j + 1) * tb)
    units = [(h, qi, p) for qi in range(seq // tb) for h in range(ATT_HEADS) for p in range(2)]
    slot = lambda unit: units.index(unit) % (ATT_PIPE + 1)
    q_halves, col_max, normed = {}, {}, {}

    def logits(h, qi, p):
        if (h, qi) not in q_halves:
            q = (q_ref[0, blocks(qi), head_cols(h)].astype(F32) * (DA_HEAD_DIM ** -0.5 * LOG2E)).astype(BF16)
            q_halves[h, qi] = (jnp.where(lane < DA_HEAD_DIM, q, jnp.zeros_like(q)),
                               jnp.where(lane >= DA_HEAD_DIM, q, jnp.zeros_like(q)))
        s = _dot_nt(k_ref[0, 0:(qi + 1) * tb, head_cols(h)], q_halves[h, qi][p])
        for j in range(qi + 1):
            sj = s[blocks(j)]
            if j >= qi - 1:
                sj = sj + bias_ref[h, qi - j]
            s_ref[slot((h, qi, p)), blocks(j), :] = sj

    def finish_logits(h, qi, p):
        m = None
        for j in range(qi + 1):
            bm = jnp.max(s_ref[slot((h, qi, p)), blocks(j), :].reshape(tb // 8, 8, tb), axis=0)
            m = bm if m is None else jnp.maximum(m, bm)
        col_max[h, qi, p] = jnp.max(m, axis=0, keepdims=True)

    def probs(h, qi, p):
        acc = None
        for j in range(qi + 1):
            pt = jnp.exp2((s_ref[slot((h, qi, p)), blocks(j), :] - col_max[h, qi, p]).astype(BF16))
            ba = _dot(vt_ref[h, :, blocks(j)], pt)
            acc = ba if acc is None else acc + ba
        normed[h, qi, p] = acc[0:HEAD_W] * (1.0 / acc[HEAD_W:HEAD_W + 1])

    def epilogue(h, qi):
        o = normed.pop((h, qi, 0)) - lam * normed.pop((h, qi, 1))
        ms = jnp.mean(o * o, axis=0, keepdims=True)
        on = (o * lax.rsqrt(ms + SUBLN_EPS)).T
        z = z_ref[0, blocks(qi), head_cols(h)].astype(F32)
        y = on * subln_ref[...] * (1.0 - lambda_init) * (z * _sigmoid(z))
        o_ref[0, blocks(qi), head_cols(h)] = y.astype(o_ref.dtype)

    for r in range(len(units) + ATT_PIPE):
        unit = units[r] if r < len(units) else None
        late = units[r - ATT_PIPE] if r >= ATT_PIPE else None
        if unit is not None:
            logits(*unit)
        if late is not None:
            probs(*late)
        if unit is not None:
            finish_logits(*unit)
        if late is not None and late[2] == 1:
            epilogue(late[0], late[1])


def _attention(proj3, bias_tiles, lam, subln_w, lambda_init):
    b, s, _ = proj3.shape
    width = ATT_HEADS * HEAD_W

    def col(block):
        return pl.BlockSpec((1, s, width), lambda bi, hi: (bi, 0, block * (N_HEADS // ATT_HEADS) + hi))

    return pl.pallas_call(
        functools.partial(_attn_kernel, seq=s, lambda_init=lambda_init),
        grid=(b, N_HEADS // ATT_HEADS),
        in_specs=[
            pl.BlockSpec((4, DA_HEAD_DIM), lambda bi, hi: (0, 0)),
            pl.BlockSpec((1, HEAD_W), lambda bi, hi: (0, 0)),
            pl.BlockSpec((ATT_HEADS, 2, ATT_BLOCK, ATT_BLOCK), lambda bi, hi: (hi, 0, 0, 0)),
            col(COL_DA_Q), col(COL_DA_K), col(COL_DA_V), col(COL_DA_Z),
        ],
        out_specs=pl.BlockSpec((1, s, width), lambda bi, hi: (bi, 0, hi)),
        out_shape=jax.ShapeDtypeStruct((b, s, D_MODEL), BF16),
        scratch_shapes=[
            pltpu.VMEM((ATT_HEADS, HEAD_W + ONES_ROWS, s), BF16),
            pltpu.VMEM((ATT_PIPE + 1, s, ATT_BLOCK), F32),
        ],
        compiler_params=pltpu.CompilerParams(
            dimension_semantics=("arbitrary", "arbitrary"), vmem_limit_bytes=VMEM_LIMIT),
        name="diff_attention",
    )(lam, subln_w, bias_tiles, proj3, proj3, proj3, proj3)


_BCAST_PARTS = (3, 3, 2, 2)
_DECAY_PARTS = 2


_BCAST_W = len(_BCAST_PARTS) * HEAD_W


def _bcast_select_np():
    e = np.zeros((HEAD_W, N_HEADS * _BCAST_W), np.float32)
    g = 0
    for o, parts in enumerate(_BCAST_PARTS):
        for _ in range(parts):
            for h in range(N_HEADS):
                e[8 * g + h, h * _BCAST_W + o * HEAD_W:h * _BCAST_W + (o + 1) * HEAD_W] = 1.0
            g += 1
    return e


def _row_select_np():
    e = np.zeros((HEAD_W, N_HEADS * HEAD_W), np.float32)
    for g in range(_DECAY_PARTS):
        for h in range(N_HEADS):
            e[8 * g + h, h * HEAD_W:(h + 1) * HEAD_W] = 1.0
    return e


def _split_parts(x, n):
    parts = []
    r = x
    for i in range(n):
        p = r.astype(BF16).astype(F32)
        parts.append(p)
        if i + 1 < n:
            r = r - p
    return parts


def _pack_groups(arrays_and_parts, group):
    parts = []
    for x, n in arrays_and_parts:
        parts.extend(_split_parts(x, n))
    packed = jnp.zeros_like(parts[0])
    for gi, p in enumerate(parts):
        packed = jnp.where(group == gi, p, packed)
    return packed.astype(BF16)


def _mlstm_kernel(q_ref, k_ref, v_ref, g_ref, bif_ref, mhw_ref, ltri_ref, esel_ref, rsel_ref, y_ref,
                  bc_ref, wpt_ref, dcrow_ref, c_ref, m_ref, *, tblk):
    L = ML_CHUNK
    n_chunks = tblk // L

    @pl.when(pl.program_id(1) == 0)
    def _():
        c_ref[...] = jnp.zeros_like(c_ref)
        m_ref[...] = jnp.zeros_like(m_ref)

    row = lax.broadcasted_iota(jnp.int32, (tblk, HEAD_W), 0) % L
    group = lax.broadcasted_iota(jnp.int32, (tblk, HEAD_W), 1) // 8
    ig = g_ref[0, :, 0:HEAD_W] + bif_ref[0:1, :]
    fg = g_ref[0, :, HEAD_W:2 * HEAD_W] + bif_ref[1:2, :]
    lf = jnp.minimum(fg, 0.0) - jnp.log(1.0 + jnp.exp(-jnp.abs(fg)))
    ltri = ltri_ref[...]
    lf_parts = [p.astype(BF16) for p in _split_parts(lf, 3)]
    bcum = jnp.concatenate(
        [sum(_dot(ltri, p[c * L:(c + 1) * L]) for p in lf_parts) for c in range(n_chunks)], axis=0)
    wp = ig - bcum
    cmax = wp
    sh = 1
    while sh < L:
        cmax = jnp.where(row >= sh, jnp.maximum(cmax, pltpu.roll(cmax, sh, 0)), cmax)
        sh *= 2
    m_rows, b_last_rows, m_next_rows = [], [], []
    m_state = m_ref[...]
    for c in range(n_chunks):
        last = (c + 1) * L - 1
        b_last = bcum[last:last + 1, :]
        m_next = b_last + jnp.maximum(m_state, cmax[last:last + 1, :])
        dcrow_ref[c] = _dot(
            _pack_groups([(jnp.broadcast_to(jnp.exp(b_last + m_state - m_next), (8, HEAD_W)), _DECAY_PARTS)],
                         group[0:8]), rsel_ref[...])
        m_rows.append(jnp.broadcast_to(m_state, (L, HEAD_W)))
        b_last_rows.append(jnp.broadcast_to(b_last, (L, HEAD_W)))
        m_next_rows.append(jnp.broadcast_to(m_next, (L, HEAD_W)))
        m_state = m_next
    m_ref[...] = m_state
    m_prev = jnp.concatenate(m_rows, axis=0)
    inter = bcum + m_prev
    m_t = jnp.maximum(inter, bcum + cmax)
    w_inter = jnp.exp(inter - m_t)
    w_state = jnp.exp(jnp.concatenate(b_last_rows, axis=0) + wp - jnp.concatenate(m_next_rows, axis=0))
    k_scale = HEAD_W ** -0.5
    bc_ref[...] = _dot(_pack_groups(
        [((bcum - m_t) * LOG2E + math.log2(k_scale), 3), (m_t * -LOG2E, 3), (w_inter * k_scale, 2),
         (w_state, 2)], group), esel_ref[...])
    wp2 = wp * LOG2E
    for c in range(n_chunks):
        wpt_ref[c] = wp2[c * L:(c + 1) * L].T

    causal = (lax.broadcasted_iota(jnp.int32, (L, HEAD_W), 1)
              <= lax.broadcasted_iota(jnp.int32, (L, HEAD_W), 0))
    ones_blk = jnp.ones((L, HEAD_W), BF16)

    def chunk(c, _):
        rows = pl.ds(pl.multiple_of(c * L, L), L)
        wp_t = wpt_ref[c]
        dc_rows = dcrow_ref[c]
        heads = [dict(hs=slice(h * HEAD_W, (h + 1) * HEAD_W)) for h in range(N_HEADS)]

        def bcast(h, o):
            return bc_ref[rows, h * _BCAST_W + o * HEAD_W:h * _BCAST_W + (o + 1) * HEAD_W]

        def stage_scores(h):
            st = heads[h]
            st["q"], st["k"], st["v"] = (ref[0, rows, st["hs"]] for ref in (q_ref, k_ref, v_ref))
            st["s"] = _dot_nt(st["q"], st["k"])
            ws_b = bcast(h, 3)
            wv = jnp.concatenate([(ws_b * st["v"].astype(F32)).astype(BF16), ws_b.astype(BF16)], axis=1)
            st["upd"] = _dot_tn(st["k"], wv)

        def stage_readout(h):
            st = heads[h]
            w_intra = jnp.exp2(jnp.where(causal, bcast(h, 0) + wp_t[h:h + 1, :], NEG_INF))
            scores = (st["s"] * w_intra).astype(BF16)
            lhs = jnp.concatenate([scores, (st["q"].astype(F32) * bcast(h, 2)).astype(BF16)], axis=1)
            st["c_old"] = c_ref[h]
            rhs = jnp.concatenate(
                [jnp.concatenate([st["v"], ones_blk], axis=1), st["c_old"].astype(BF16)], axis=0)
            st["tot"] = _dot(lhs, rhs)

        def stage_finish(h):
            st = heads[h]
            tot = st["tot"]
            hh = tot[:, 0:HEAD_W] / jnp.maximum(jnp.abs(tot[:, HEAD_W:]), jnp.exp2(bcast(h, 1)))
            mu = jnp.mean(hh, axis=-1, keepdims=True)
            hc = hh - mu
            var = jnp.mean(hc * hc, axis=-1, keepdims=True)
            y_ref[0, rows, st["hs"]] = (
                hc * lax.rsqrt(var + HEAD_LN_EPS) * mhw_ref[:, st["hs"]]).astype(y_ref.dtype)
            dc = dc_rows[0:1, st["hs"]]
            c_ref[h] = jnp.concatenate([dc, dc], axis=1) * st["c_old"] + st["upd"]
            st.clear()

        for i in range(N_HEADS + 2 * HEAD_PIPE):
            if i < N_HEADS:
                stage_scores(i)
            if 0 <= i - HEAD_PIPE < N_HEADS:
                stage_readout(i - HEAD_PIPE)
            if 0 <= i - 2 * HEAD_PIPE < N_HEADS:
                stage_finish(i - 2 * HEAD_PIPE)
        return 0

    lax.fori_loop(0, n_chunks, chunk, 0)


def _mlstm(proj3, gates3, bif, mh_w, tblk):
    b, s, _ = proj3.shape
    n_chunks = tblk // ML_CHUNK

    def col(block):
        return pl.BlockSpec((1, tblk, D_MODEL), lambda bi, ti: (bi, ti, block))

    def full(shape):
        return pl.BlockSpec(shape, lambda bi, ti: (0,) * len(shape))

    ltri = jnp.asarray(np.tril(np.ones((ML_CHUNK, ML_CHUNK), np.float32)), BF16)
    esel = jnp.asarray(_bcast_select_np(), BF16)
    rsel = jnp.asarray(_row_select_np(), BF16)
    return pl.pallas_call(
        functools.partial(_mlstm_kernel, tblk=tblk),
        grid=(b, s // tblk),
        in_specs=[
            col(COL_ML_Q), col(COL_ML_K), col(COL_ML_V),
            pl.BlockSpec((1, tblk, 2 * HEAD_W), lambda bi, ti: (bi, ti, 0)),
            full((2, HEAD_W)), full((1, D_MODEL)),
            full((ML_CHUNK, ML_CHUNK)), full(esel.shape), full(rsel.shape),
        ],
        out_specs=pl.BlockSpec((1, tblk, D_MODEL), lambda bi, ti: (bi, ti, 0)),
        out_shape=jax.ShapeDtypeStruct((b, s, D_MODEL), BF16),
        scratch_shapes=[
            pltpu.VMEM((tblk, N_HEADS * _BCAST_W), F32),
            pltpu.VMEM((n_chunks, HEAD_W, ML_CHUNK), F32),
            pltpu.VMEM((n_chunks, 8, D_MODEL), F32),
            pltpu.VMEM((N_HEADS, HEAD_W, 2 * HEAD_W), F32),
            pltpu.VMEM((1, HEAD_W), F32),
        ],
        compiler_params=pltpu.CompilerParams(
            dimension_semantics=("arbitrary", "arbitrary"), vmem_limit_bytes=VMEM_LIMIT),
        name="mlstm",
    )(proj3, proj3, proj3, gates3, bif, mh_w, ltri, esel, rsel)


def _out_kernel(ya_ref, hn_ref, og_ref, zg_ref, ga_ref, gm_ref, x_ref, wpa_ref, wpm_ref, wout_ref, bg_ref,
                nf_ref, o_ref, *, final_norm):
    branches = {}

    def project_branches(r):
        rows = slice(r, r + OUT_ROWS)
        zg = zg_ref[rows, :].astype(F32)
        y_m = ((_sigmoid(og_ref[rows, :].astype(F32)) * hn_ref[rows, :].astype(F32))
               * (zg * _sigmoid(zg))).astype(BF16)
        branches[r] = (_dot(ya_ref[rows, :], wpa_ref[...]), _dot(y_m, wpm_ref[...]))

    def merge_and_project(r):
        rows = slice(r, r + OUT_ROWS)
        a, b = branches.pop(r)
        g_a = _sigmoid(ga_ref[rows, :].astype(F32) + bg_ref[0:1, :])
        g_m = _sigmoid(gm_ref[rows, :].astype(F32) + bg_ref[1:2, :])
        hres = x_ref[rows, :] + _dot((g_a * a + g_m * b).astype(BF16), wout_ref[...])
        if final_norm:
            ms = jnp.mean(hres * hres, axis=-1, keepdims=True)
            hres = hres * lax.rsqrt(ms + NORM_EPS) * nf_ref[...]
        o_ref[rows, :] = hres

    tm = o_ref.shape[0]
    project_branches(0)
    for r in range(0, tm, OUT_ROWS):
        if r + OUT_ROWS < tm:
            project_branches(r + OUT_ROWS)
        merge_and_project(r)


def _out_stage(y_a, h_n, proj, x2, w_pa, w_pm, w_out, b_gate, norm_final, tm, final_norm):
    m = x2.shape[0]

    def rows(block=0):
        return pl.BlockSpec((tm, D_MODEL), lambda i: (i, block))

    def full(shape):
        return pl.BlockSpec(shape, lambda i: (0,) * len(shape))

    return pl.pallas_call(
        functools.partial(_out_kernel, final_norm=final_norm),
        grid=(m // tm,),
        in_specs=[
            rows(), rows(), rows(COL_ML_O), rows(COL_ML_Z), rows(COL_GATE_A), rows(COL_GATE_M), rows(),
            full((D_MODEL, D_MODEL)), full((D_MODEL, D_MODEL)), full((D_MODEL, D_MODEL)),
            full((2, D_MODEL)), full((1, D_MODEL)),
        ],
        out_specs=rows(),
        out_shape=jax.ShapeDtypeStruct((m, D_MODEL), F32),
        compiler_params=pltpu.CompilerParams(
            dimension_semantics=("arbitrary",), vmem_limit_bytes=VMEM_LIMIT),
        name="out_stage",
    )(y_a, h_n, proj, proj, proj, proj, x2, w_pa, w_pm, w_out, b_gate, norm_final)


def _transposed_in_weight(w_in, layer):
    w_t = jnp.swapaxes(w_in, 1, 2)
    n_wide = N_MAIN_BLOCKS * D_MODEL
    tail_t = w_t[layer, n_wide + 2 * N_HEADS:, :].astype(BF16)
    reps = HEAD_W // N_HEADS
    gate_t = jnp.concatenate([jnp.tile(w_t[layer, n_wide:n_wide + N_HEADS, :], (reps, 1)),
                              jnp.tile(w_t[layer, n_wide + N_HEADS:n_wide + 2 * N_HEADS, :], (reps, 1))],
                             axis=0).astype(BF16)
    return w_t, tail_t, gate_t


def kernel(x, norm_w, w_in, lam, subln_w, rel_bias, conv_w, conv_b, b_if, mh_w, b_gate, w_pa, w_pm, w_out,
           norm_final):
    b, s, d = x.shape
    depth = norm_w.shape[0]
    m = b * s
    tm_out = min(1024, m)
    tblk = min(512, s)
    bias_tiles = _bias_tiles(rel_bias)
    h2 = x.reshape(m, d)
    for layer in range(depth):
        lambda_init = 0.8 - 0.6 * math.exp(-0.3 * layer)
        w_t, tail_t, gate_t = _transposed_in_weight(w_in, layer)
        proj, gates = _inproj(h2, norm_w[layer][None, :], w_t, layer, tail_t, gate_t,
                              conv_w[layer], conv_b[layer], s)
        proj3 = proj.reshape(b, s, N_COL_BLOCKS * D_MODEL)
        y_a = _attention(proj3, bias_tiles, lam[layer], subln_w[layer][None, :], lambda_init)
        bif = jnp.tile(b_if[layer], (1, HEAD_W // N_HEADS))
        h_n = _mlstm(proj3, gates.reshape(b, s, 2 * HEAD_W), bif, mh_w[layer][None, :], tblk)
        h2 = _out_stage(y_a.reshape(m, d), h_n.reshape(m, d), proj, h2,
                        w_pa[layer].astype(BF16), w_pm[layer].astype(BF16), w_out[layer].astype(BF16),
                        b_gate[layer], norm_final[None, :], tm_out, final_norm=(layer == depth - 1))
    return h2.reshape(b, s, d)
```

```python
import functools
import math

import numpy as np
import jax
import jax.numpy as jnp
from jax import lax
from jax.experimental import pallas as pl
from jax.experimental.pallas import tpu as pltpu

F32 = jnp.float32
BF16 = jnp.bfloat16

D_MODEL = 1024
N_HEADS = 8
HEAD_W = 128
DA_HEAD_DIM = 64
CHUNK = 64
CONV_K = 4
N_BUCKETS = 32
MAX_DISTANCE = 128
NORM_EPS = 1e-6
SUBLN_EPS = 1e-5
HEAD_LN_EPS = 1e-5
NEG_INF = -1e30
LOG2E = 1.4426950408889634

COL_DA_Q, COL_DA_K, COL_DA_V, COL_DA_Z = 0, 1, 2, 3
COL_ML_Q, COL_ML_K, COL_ML_V, COL_ML_O, COL_ML_Z = 4, 5, 6, 7, 8
COL_GATE_A, COL_GATE_M = 9, 10
N_COL_BLOCKS = 11
N_MAIN_BLOCKS = 9

ATT_BLOCK = 256
ONES_ROWS = 16
ATT_PIPE = 2
ATT_HEADS = 2
ML_CHUNK = 128
HEAD_PIPE = 3
OUT_ROWS = 256
VMEM_LIMIT = 56 * 1024 * 1024


def _sigmoid(x):
    return 0.5 * jnp.tanh(0.5 * x) + 0.5


def _dot(a, b):
    return jnp.dot(a, b, preferred_element_type=F32)


def _dot_nt(a, b):
    return lax.dot_general(a, b, (((1,), (1,)), ((), ())), preferred_element_type=F32)


def _dot_tn(a, b):
    return lax.dot_general(a, b, (((0,), (0,)), ((), ())), preferred_element_type=F32)


CONV_ROWS = 512
CONV_PAD = 8


def _inproj_kernel(x_ref, nw_ref, w_ref, wt_ref, wg_ref, cw_ref, cb_ref, o_ref, g_ref, xn_ref, y_ref, wb_ref,
                   *, seq):
    j = pl.program_id(1)

    @pl.when(j == 0)
    def _():
        x = x_ref[...]
        ms = jnp.mean(x * x, axis=-1, keepdims=True)
        xn = (x * lax.rsqrt(ms + NORM_EPS) * nw_ref[...]).astype(BF16)
        xn_ref[...] = xn
        g_ref[...] = _dot_nt(xn, wg_ref[...])

    is_conv = (j == COL_ML_Q) | (j == COL_ML_K)

    @pl.when(jnp.logical_not(is_conv) & (j < N_MAIN_BLOCKS))
    def _():
        o_ref[...] = _dot_nt(xn_ref[...], w_ref[...].astype(BF16)).astype(BF16)

    @pl.when(j >= N_MAIN_BLOCKS)
    def _():
        o_ref[...] = _dot_nt(xn_ref[...], wt_ref[...]).astype(BF16)

    @pl.when(is_conv)
    def _():
        which = j - COL_ML_Q
        taps = cw_ref[which]
        bias = cb_ref[pl.ds(which, 1), :]

        y_ref[0:CONV_PAD, :] = jnp.zeros((CONV_PAD, D_MODEL), F32)
        wb_ref[...] = w_ref[...].astype(BF16)
        for r in range(0, seq, CONV_ROWS):
            y_ref[CONV_PAD:, :] = _dot_nt(xn_ref[r:r + CONV_ROWS, :], wb_ref[...])
            win = y_ref[...]
            acc = bias + taps[CONV_K - 1:CONV_K, :] * win[CONV_PAD:, :]
            for back in range(1, CONV_K):
                shifted = pltpu.roll(win, back, 0)[CONV_PAD:, :]
                acc = acc + taps[CONV_K - 1 - back:CONV_K - back, :] * shifted
            o_ref[r:r + CONV_ROWS, :] = (acc * _sigmoid(acc)).astype(BF16)
            y_ref[0:CONV_PAD, :] = win[CONV_ROWS:, :]


def _inproj(x2, norm_w, w_t, layer, tail_t, gate_t, conv_w, conv_b, seq):
    m = x2.shape[0]
    n = N_COL_BLOCKS * D_MODEL
    return pl.pallas_call(
        functools.partial(_inproj_kernel, seq=seq),
        grid=(m // seq, N_COL_BLOCKS),
        in_specs=[
            pl.BlockSpec((seq, D_MODEL), lambda i, j: (i, 0)),
            pl.BlockSpec((1, D_MODEL), lambda i, j: (0, 0)),
            pl.BlockSpec((None, D_MODEL, D_MODEL), lambda i, j: (layer, jnp.minimum(j, N_MAIN_BLOCKS - 1), 0)),
            pl.BlockSpec((D_MODEL, D_MODEL), lambda i, j: (jnp.maximum(j - N_MAIN_BLOCKS, 0), 0)),
            pl.BlockSpec((2 * HEAD_W, D_MODEL), lambda i, j: (0, 0)),
            pl.BlockSpec((2, CONV_K, D_MODEL), lambda i, j: (0, 0, 0)),
            pl.BlockSpec((2, D_MODEL), lambda i, j: (0, 0)),
        ],
        out_specs=[
            pl.BlockSpec((seq, D_MODEL), lambda i, j: (i, j)),
            pl.BlockSpec((seq, 2 * HEAD_W), lambda i, j: (i, 0)),
        ],
        out_shape=[
            jax.ShapeDtypeStruct((m, n), BF16),
            jax.ShapeDtypeStruct((m, 2 * HEAD_W), F32),
        ],
        scratch_shapes=[
            pltpu.VMEM((seq, D_MODEL), BF16),
            pltpu.VMEM((CONV_ROWS + CONV_PAD, D_MODEL), F32),
            pltpu.VMEM((D_MODEL, D_MODEL), BF16),
        ],
        compiler_params=pltpu.CompilerParams(
            dimension_semantics=("arbitrary", "arbitrary"), vmem_limit_bytes=VMEM_LIMIT),
        name="inproj",
    )(x2, norm_w, w_t, tail_t, gate_t, conv_w, conv_b)


def _rel_bucket_np(rel):
    nb = N_BUCKETS // 2
    max_exact = nb // 2
    bucket = np.where(rel > 0, nb, 0)
    n = np.abs(rel)
    nf = np.maximum(n, 1).astype(np.float32)
    large = max_exact + (np.log(nf / np.float32(max_exact)) / np.float32(math.log(MAX_DISTANCE / max_exact))
                         * np.float32(nb - max_exact)).astype(np.int32)
    large = np.minimum(large, nb - 1)
    return bucket + np.where(n < max_exact, n, large)


def _bucket_tiles():
    dk = np.arange(ATT_BLOCK)[:, None]
    dq = np.arange(ATT_BLOCK)[None, :]
    diag = np.where(dk // CHUNK <= dq // CHUNK, _rel_bucket_np(dk - dq), -1)
    prev = _rel_bucket_np(dk - dq - ATT_BLOCK)
    return np.stack([diag, prev]).astype(np.int32)


FAR_BUCKET = int(_rel_bucket_np(np.array([-(ATT_BLOCK + 1)]))[0])


def _bias_kernel(rb_ref, bkt_ref, o_ref):
    h = pl.program_id(0)
    c_far = rb_ref[FAR_BUCKET, h]
    for t in range(2):
        bk = bkt_ref[t]
        acc = jnp.full((ATT_BLOCK, ATT_BLOCK), NEG_INF, F32)
        for n in range(N_BUCKETS):
            acc = jnp.where(bk == n, (rb_ref[n, h] - c_far) * LOG2E, acc)
        o_ref[0, t] = acc


def _bias_tiles(rel_bias):
    return pl.pallas_call(
        _bias_kernel,
        grid=(N_HEADS,),
        in_specs=[
            pl.BlockSpec(memory_space=pltpu.SMEM),
            pl.BlockSpec((2, ATT_BLOCK, ATT_BLOCK), lambda h: (0, 0, 0)),
        ],
        out_specs=pl.BlockSpec((1, 2, ATT_BLOCK, ATT_BLOCK), lambda h: (h, 0, 0, 0)),
        out_shape=jax.ShapeDtypeStruct((N_HEADS, 2, ATT_BLOCK, ATT_BLOCK), F32),
        name="bias_tiles",
    )(rel_bias, jnp.asarray(_bucket_tiles()))


def _attn_kernel(lam_ref, subln_ref, bias_ref, q_ref, k_ref, v_ref, z_ref, o_ref,
                 vt_ref, s_ref, *, seq, lambda_init):
    tb = ATT_BLOCK
    lamv = lam_ref[...]
    lam = (jnp.exp(jnp.sum(lamv[0:1] * lamv[1:2], axis=-1, keepdims=True))
           - jnp.exp(jnp.sum(lamv[2:3] * lamv[3:4], axis=-1, keepdims=True)) + lambda_init)
    head_cols = lambda h: slice(h * HEAD_W, (h + 1) * HEAD_W)

    for h in range(ATT_HEADS):
        for i in range(seq // HEAD_W):
            sl = slice(i * HEAD_W, (i + 1) * HEAD_W)
            vt_ref[h, 0:HEAD_W, sl] = v_ref[0, sl, head_cols(h)].astype(F32).T.astype(BF16)
        vt_ref[h, HEAD_W:, :] = jnp.ones((ONES_ROWS, seq), BF16)

    lane = lax.broadcasted_iota(jnp.int32, (tb, HEAD_W), 1)
    blocks = lambda j: slice(j * tb, (j + 1) * tb)
    units = [(h, qi, p) for qi in range(seq // tb) for h in range(ATT_HEADS) for p in range(2)]
    slot = lambda unit: units.index(unit) % (ATT_PIPE + 1)
    q_halves, col_max, normed = {}, {}, {}

    def logits(h, qi, p):
        if (h, qi) not in q_halves:
            q = (q_ref[0, blocks(qi), head_cols(h)].astype(F32) * (DA_HEAD_DIM ** -0.5 * LOG2E)).astype(BF16)
            q_halves[h, qi] = (jnp.where(lane < DA_HEAD_DIM, q, jnp.zeros_like(q)),
                               jnp.where(lane >= DA_HEAD_DIM, q, jnp.zeros_like(q)))
        s = _dot_nt(k_ref[0, 0:(qi + 1) * tb, head_cols(h)], q_halves[h, qi][p])
        for j in range(qi + 1):
            sj = s[blocks(j)]
            if j >= qi - 1:
                sj = sj + bias_ref[h, qi - j]
            s_ref[slot((h, qi, p)), blocks(j), :] = sj

    def finish_logits(h, qi, p):
        m = None
        for j in range(qi + 1):
            bm = jnp.max(s_ref[slot((h, qi, p)), blocks(j), :].reshape(tb // 8, 8, tb), axis=0)
            m = bm if m is None else jnp.maximum(m, bm)
        col_max[h, qi, p] = jnp.max(m, axis=0, keepdims=True)

    def probs(h, qi, p):
        acc = None
        for j in range(qi + 1):
            pt = jnp.exp2((s_ref[slot((h, qi, p)), blocks(j), :] - col_max[h, qi, p]).astype(BF16))
            ba = _dot(vt_ref[h, :, blocks(j)], pt)
            acc = ba if acc is None else acc + ba
        normed[h, qi, p] = acc[0:HEAD_W] * (1.0 / acc[HEAD_W:HEAD_W + 1])

    def epilogue(h, qi):
        o = normed.pop((h, qi, 0)) - lam * normed.pop((h, qi, 1))
        ms = jnp.mean(o * o, axis=0, keepdims=True)
        on = (o * lax.rsqrt(ms + SUBLN_EPS)).T
        z = z_ref[0, blocks(qi), head_cols(h)].astype(F32)
        y = on * subln_ref[...] * (1.0 - lambda_init) * (z * _sigmoid(z))
        o_ref[0, blocks(qi), head_cols(h)] = y.astype(o_ref.dtype)

    for r in range(len(units) + ATT_PIPE):
        unit = units[r] if r < len(units) else None
        late = units[r - ATT_PIPE] if r >= ATT_PIPE else None
        if unit is not None:
            logits(*unit)
        if late is not None:
            probs(*late)
        if unit is not None:
            finish_logits(*unit)
        if late is not None and late[2] == 1:
            epilogue(late[0], late[1])


def _attention(proj3, bias_tiles, lam, subln_w, lambda_init):
    b, s, _ = proj3.shape
    width = ATT_HEADS * HEAD_W

    def col(block):
        return pl.BlockSpec((1, s, width), lambda bi, hi: (bi, 0, block * (N_HEADS // ATT_HEADS) + hi))

    return pl.pallas_call(
        functools.partial(_attn_kernel, seq=s, lambda_init=lambda_init),
        grid=(b, N_HEADS // ATT_HEADS),
        in_specs=[
            pl.BlockSpec((4, DA_HEAD_DIM), lambda bi, hi: (0, 0)),
            pl.BlockSpec((1, HEAD_W), lambda bi, hi: (0, 0)),
            pl.BlockSpec((ATT_HEADS, 2, ATT_BLOCK, ATT_BLOCK), lambda bi, hi: (hi, 0, 0, 0)),
            col(COL_DA_Q), col(COL_DA_K), col(COL_DA_V), col(COL_DA_Z),
        ],
        out_specs=pl.BlockSpec((1, s, width), lambda bi, hi: (bi, 0, hi)),
        out_shape=jax.ShapeDtypeStruct((b, s, D_MODEL), BF16),
        scratch_shapes=[
            pltpu.VMEM((ATT_HEADS, HEAD_W + ONES_ROWS, s), BF16),
            pltpu.VMEM((ATT_PIPE + 1, s, ATT_BLOCK), F32),
        ],
        compiler_params=pltpu.CompilerParams(
            dimension_semantics=("arbitrary", "arbitrary"), vmem_limit_bytes=VMEM_LIMIT),
        name="diff_attention",
    )(lam, subln_w, bias_tiles, proj3, proj3, proj3, proj3)


_BCAST_PARTS = (3, 3, 2)
_DECAY_PARTS = 2


_BCAST_W = len(_BCAST_PARTS) * HEAD_W


def _bcast_select_np():
    e = np.zeros((HEAD_W, N_HEADS * _BCAST_W), np.float32)
    g = 0
    for o, parts in enumerate(_BCAST_PARTS):
        for _ in range(parts):
            for h in range(N_HEADS):
                e[8 * g + h, h * _BCAST_W + o * HEAD_W:h * _BCAST_W + (o + 1) * HEAD_W] = 1.0
            g += 1
    return e


def _row_select_np():
    e = np.zeros((HEAD_W, N_HEADS * HEAD_W), np.float32)
    for g in range(_DECAY_PARTS):
        for h in range(N_HEADS):
            e[8 * g + h, h * HEAD_W:(h + 1) * HEAD_W] = 1.0
    return e


def _split_parts(x, n):
    parts = []
    r = x
    for i in range(n):
        p = r.astype(BF16).astype(F32)
        parts.append(p)
        if i + 1 < n:
            r = r - p
    return parts


def _pack_groups(arrays_and_parts, group):
    parts = []
    for x, n in arrays_and_parts:
        parts.extend(_split_parts(x, n))
    packed = jnp.zeros_like(parts[0])
    for gi, p in enumerate(parts):
        packed = jnp.where(group == gi, p, packed)
    return packed.astype(BF16)


def _mlstm_kernel(q_ref, k_ref, v_ref, g_ref, bif_ref, mhw_ref, ltri_ref, esel_ref, rsel_ref, y_ref,
                  bc_ref, wpt_ref, wst_ref, dcrow_ref, c_ref, m_ref, *, tblk):
    L = ML_CHUNK
    n_chunks = tblk // L

    @pl.when(pl.program_id(1) == 0)
    def _():
        c_ref[...] = jnp.zeros_like(c_ref)
        m_ref[...] = jnp.zeros_like(m_ref)

    row = lax.broadcasted_iota(jnp.int32, (tblk, HEAD_W), 0) % L
    group = lax.broadcasted_iota(jnp.int32, (tblk, HEAD_W), 1) // 8
    ig = g_ref[0, :, 0:HEAD_W] + bif_ref[0:1, :]
    fg = g_ref[0, :, HEAD_W:2 * HEAD_W] + bif_ref[1:2, :]
    lf = jnp.minimum(fg, 0.0) - jnp.log(1.0 + jnp.exp(-jnp.abs(fg)))
    ltri = ltri_ref[...]
    lf_parts = [p.astype(BF16) for p in _split_parts(lf, 3)]
    bcum = jnp.concatenate(
        [sum(_dot(ltri, p[c * L:(c + 1) * L]) for p in lf_parts) for c in range(n_chunks)], axis=0)
    wp = ig - bcum
    cmax = wp
    sh = 1
    while sh < L:
        cmax = jnp.where(row >= sh, jnp.maximum(cmax, pltpu.roll(cmax, sh, 0)), cmax)
        sh *= 2
    m_rows, b_last_rows, m_next_rows = [], [], []
    m_state = m_ref[...]
    for c in range(n_chunks):
        last = (c + 1) * L - 1
        b_last = bcum[last:last + 1, :]
        m_next = b_last + jnp.maximum(m_state, cmax[last:last + 1, :])
        dcrow_ref[c] = _dot(
            _pack_groups([(jnp.broadcast_to(jnp.exp(b_last + m_state - m_next), (8, HEAD_W)), _DECAY_PARTS)],
                         group[0:8]), rsel_ref[...])
        m_rows.append(jnp.broadcast_to(m_state, (L, HEAD_W)))
        b_last_rows.append(jnp.broadcast_to(b_last, (L, HEAD_W)))
        m_next_rows.append(jnp.broadcast_to(m_next, (L, HEAD_W)))
        m_state = m_next
    m_ref[...] = m_state
    m_prev = jnp.concatenate(m_rows, axis=0)
    inter = bcum + m_prev
    m_t = jnp.maximum(inter, bcum + cmax)
    w_inter = jnp.exp(inter - m_t)
    w_state = jnp.exp(jnp.concatenate(b_last_rows, axis=0) + wp - jnp.concatenate(m_next_rows, axis=0))
    k_scale = HEAD_W ** -0.5
    bc_ref[...] = _dot(_pack_groups(
        [((bcum - m_t) * LOG2E + math.log2(k_scale), 3), (m_t * -LOG2E, 3), (w_inter * k_scale, 2)], group),
        esel_ref[...])
    wp2 = wp * LOG2E
    for c in range(n_chunks):
        wpt_ref[c] = wp2[c * L:(c + 1) * L].T
        wst_ref[c] = w_state[c * L:(c + 1) * L].T

    causal = (lax.broadcasted_iota(jnp.int32, (L, HEAD_W), 1)
              <= lax.broadcasted_iota(jnp.int32, (L, HEAD_W), 0))
    ones_blk = jnp.ones((L, HEAD_W), BF16)

    def chunk(c, _):
        rows = pl.ds(pl.multiple_of(c * L, L), L)
        wp_t = wpt_ref[c]
        ws_t = wst_ref[c]
        dc_rows = dcrow_ref[c]
        heads = [dict(hs=slice(h * HEAD_W, (h + 1) * HEAD_W)) for h in range(N_HEADS)]

        def bcast(h, o):
            return bc_ref[rows, h * _BCAST_W + o * HEAD_W:h * _BCAST_W + (o + 1) * HEAD_W]

        def stage_scores(h):
            st = heads[h]
            st["q"], st["k"], st["v"] = (ref[0, rows, st["hs"]] for ref in (q_ref, k_ref, v_ref))
            st["s"] = _dot_nt(st["q"], st["k"])
            k_t = (st["k"].astype(F32).T * ws_t[h:h + 1, :]).astype(BF16)
            st["upd"] = _dot(k_t, jnp.concatenate([st["v"], ones_blk], axis=1))

        def stage_readout(h):
            st = heads[h]
            w_intra = jnp.exp2(jnp.where(causal, bcast(h, 0) + wp_t[h:h + 1, :], NEG_INF))
            scores = (st["s"] * w_intra).astype(BF16)
            lhs = jnp.concatenate([scores, (st["q"].astype(F32) * bcast(h, 2)).astype(BF16)], axis=1)
            st["c_old"] = c_ref[h]
            rhs = jnp.concatenate(
                [jnp.concatenate([st["v"], ones_blk], axis=1), st["c_old"].astype(BF16)], axis=0)
            st["tot"] = _dot(lhs, rhs)

        def stage_finish(h):
            st = heads[h]
            tot = st["tot"]
            hh = tot[:, 0:HEAD_W] / jnp.maximum(jnp.abs(tot[:, HEAD_W:]), jnp.exp2(bcast(h, 1)))
            mu = jnp.mean(hh, axis=-1, keepdims=True)
            hc = hh - mu
            var = jnp.mean(hc * hc, axis=-1, keepdims=True)
            y_ref[0, rows, st["hs"]] = (
                hc * lax.rsqrt(var + HEAD_LN_EPS) * mhw_ref[:, st["hs"]]).astype(y_ref.dtype)
            dc = dc_rows[0:1, st["hs"]]
            c_ref[h] = jnp.concatenate([dc, dc], axis=1) * st["c_old"] + st["upd"]
            st.clear()

        for i in range(N_HEADS + 2 * HEAD_PIPE):
            if i < N_HEADS:
                stage_scores(i)
            if 0 <= i - HEAD_PIPE < N_HEADS:
                stage_readout(i - HEAD_PIPE)
            if 0 <= i - 2 * HEAD_PIPE < N_HEADS:
                stage_finish(i - 2 * HEAD_PIPE)
        return 0

    lax.fori_loop(0, n_chunks, chunk, 0)


def _mlstm(proj3, gates3, bif, mh_w, tblk):
    b, s, _ = proj3.shape
    n_chunks = tblk // ML_CHUNK

    def col(block):
        return pl.BlockSpec((1, tblk, D_MODEL), lambda bi, ti: (bi, ti, block))

    def full(shape):
        return pl.BlockSpec(shape, lambda bi, ti: (0,) * len(shape))

    ltri = jnp.asarray(np.tril(np.ones((ML_CHUNK, ML_CHUNK), np.float32)), BF16)
    esel = jnp.asarray(_bcast_select_np(), BF16)
    rsel = jnp.asarray(_row_select_np(), BF16)
    return pl.pallas_call(
        functools.partial(_mlstm_kernel, tblk=tblk),
        grid=(b, s // tblk),
        in_specs=[
            col(COL_ML_Q), col(COL_ML_K), col(COL_ML_V),
            pl.BlockSpec((1, tblk, 2 * HEAD_W), lambda bi, ti: (bi, ti, 0)),
            full((2, HEAD_W)), full((1, D_MODEL)),
            full((ML_CHUNK, ML_CHUNK)), full(esel.shape), full(rsel.shape),
        ],
        out_specs=pl.BlockSpec((1, tblk, D_MODEL), lambda bi, ti: (bi, ti, 0)),
        out_shape=jax.ShapeDtypeStruct((b, s, D_MODEL), BF16),
        scratch_shapes=[
            pltpu.VMEM((tblk, N_HEADS * _BCAST_W), F32),
            pltpu.VMEM((n_chunks, HEAD_W, ML_CHUNK), F32),
            pltpu.VMEM((n_chunks, HEAD_W, ML_CHUNK), F32),
            pltpu.VMEM((n_chunks, 8, D_MODEL), F32),
            pltpu.VMEM((N_HEADS, HEAD_W, 2 * HEAD_W), F32),
            pltpu.VMEM((1, HEAD_W), F32),
        ],
        compiler_params=pltpu.CompilerParams(
            dimension_semantics=("arbitrary", "arbitrary"), vmem_limit_bytes=VMEM_LIMIT),
        name="mlstm",
    )(proj3, proj3, proj3, gates3, bif, mh_w, ltri, esel, rsel)


def _out_kernel(ya_ref, hn_ref, og_ref, zg_ref, ga_ref, gm_ref, x_ref, wpa_ref, wpm_ref, wout_ref, bg_ref,
                nf_ref, o_ref, *, final_norm):
    branches = {}

    def project_branches(r):
        rows = slice(r, r + OUT_ROWS)
        zg = zg_ref[rows, :].astype(F32)
        y_m = ((_sigmoid(og_ref[rows, :].astype(F32)) * hn_ref[rows, :].astype(F32))
               * (zg * _sigmoid(zg))).astype(BF16)
        branches[r] = (_dot(ya_ref[rows, :], wpa_ref[...]), _dot(y_m, wpm_ref[...]))

    def merge_and_project(r):
        rows = slice(r, r + OUT_ROWS)
        a, b = branches.pop(r)
        g_a = _sigmoid(ga_ref[rows, :].astype(F32) + bg_ref[0:1, :])
        g_m = _sigmoid(gm_ref[rows, :].astype(F32) + bg_ref[1:2, :])
        hres = x_ref[rows, :] + _dot((g_a * a + g_m * b).astype(BF16), wout_ref[...])
        if final_norm:
            ms = jnp.mean(hres * hres, axis=-1, keepdims=True)
            hres = hres * lax.rsqrt(ms + NORM_EPS) * nf_ref[...]
        o_ref[rows, :] = hres

    tm = o_ref.shape[0]
    project_branches(0)
    for r in range(0, tm, OUT_ROWS):
        if r + OUT_ROWS < tm:
            project_branches(r + OUT_ROWS)
        merge_and_project(r)


def _out_stage(y_a, h_n, proj, x2, w_pa, w_pm, w_out, b_gate, norm_final, tm, final_norm):
    m = x2.shape[0]

    def rows(block=0):
        return pl.BlockSpec((tm, D_MODEL), lambda i: (i, block))

    def full(shape):
        return pl.BlockSpec(shape, lambda i: (0,) * len(shape))

    return pl.pallas_call(
        functools.partial(_out_kernel, final_norm=final_norm),
        grid=(m // tm,),
        in_specs=[
            rows(), rows(), rows(COL_ML_O), rows(COL_ML_Z), rows(COL_GATE_A), rows(COL_GATE_M), rows(),
            full((D_MODEL, D_MODEL)), full((D_MODEL, D_MODEL)), full((D_MODEL, D_MODEL)),
            full((2, D_MODEL)), full((1, D_MODEL)),
        ],
        out_specs=rows(),
        out_shape=jax.ShapeDtypeStruct((m, D_MODEL), F32),
        compiler_params=pltpu.CompilerParams(
            dimension_semantics=("arbitrary",), vmem_limit_bytes=VMEM_LIMIT),
        name="out_stage",
    )(y_a, h_n, proj, proj, proj, proj, x2, w_pa, w_pm, w_out, b_gate, norm_final)


def _transposed_in_weight(w_in, layer):
    w_t = jnp.swapaxes(w_in, 1, 2)
    n_wide = N_MAIN_BLOCKS * D_MODEL
    tail_t = w_t[layer, n_wide + 2 * N_HEADS:, :].astype(BF16)
    reps = HEAD_W // N_HEADS
    gate_t = jnp.concatenate([jnp.tile(w_t[layer, n_wide:n_wide + N_HEADS, :], (reps, 1)),
                              jnp.tile(w_t[layer, n_wide + N_HEADS:n_wide + 2 * N_HEADS, :], (reps, 1))],
                             axis=0).astype(BF16)
    return w_t, tail_t, gate_t


def kernel(x, norm_w, w_in, lam, subln_w, rel_bias, conv_w, conv_b, b_if, mh_w, b_gate, w_pa, w_pm, w_out,
           norm_final):
    b, s, d = x.shape
    depth = norm_w.shape[0]
    m = b * s
    tm_out = min(1024, m)
    tblk = min(512, s)
    bias_tiles = _bias_tiles(rel_bias)
    h2 = x.reshape(m, d)
    for layer in range(depth):
        lambda_init = 0.8 - 0.6 * math.exp(-0.3 * layer)
        w_t, tail_t, gate_t = _transposed_in_weight(w_in, layer)
        proj, gates = _inproj(h2, norm_w[layer][None, :], w_t, layer, tail_t, gate_t,
                              conv_w[layer], conv_b[layer], s)
        proj3 = proj.reshape(b, s, N_COL_BLOCKS * D_MODEL)
        y_a = _attention(proj3, bias_tiles, lam[layer], subln_w[layer][None, :], lambda_init)
        bif = jnp.tile(b_if[layer], (1, HEAD_W // N_HEADS))
        h_n = _mlstm(proj3, gates.reshape(b, s, 2 * HEAD_W), bif, mh_w[layer][None, :], tblk)
        h2 = _out_stage(y_a.reshape(m, d), h_n.reshape(m, d), proj, h2,
                        w_pa[layer].astype(BF16), w_pm[layer].astype(BF16), w_out[layer].astype(BF16),
                        b_gate[layer], norm_final[None, :], tm_out, final_norm=(layer == depth - 1))
    return h2.reshape(b, s, d)
```

```python
import functools
import math

import numpy as np
import jax
import jax.numpy as jnp
from jax import lax
from jax.experimental import pallas as pl
from jax.experimental.pallas import tpu as pltpu

F32 = jnp.float32
BF16 = jnp.bfloat16

D_MODEL = 1024
N_HEADS = 8
HEAD_W = 128
DA_HEAD_DIM = 64
CHUNK = 64
CONV_K = 4
N_BUCKETS = 32
MAX_DISTANCE = 128
NORM_EPS = 1e-6
SUBLN_EPS = 1e-5
HEAD_LN_EPS = 1e-5
NEG_INF = -1e30
LOG2E = 1.4426950408889634

COL_DA_Q, COL_DA_K, COL_DA_V, COL_DA_Z = 0, 1, 2, 3
COL_ML_Q, COL_ML_K, COL_ML_V, COL_ML_O, COL_ML_Z = 4, 5, 6, 7, 8
COL_GATE_A, COL_GATE_M = 9, 10
N_COL_BLOCKS = 11
N_MAIN_BLOCKS = 9

ATT_BLOCK = 256
ONES_ROWS = 16
ATT_PIPE = 2
ATT_HEADS = 2
ML_CHUNK = 128
HEAD_PIPE = 1
OUT_ROWS = 256
VMEM_LIMIT = 56 * 1024 * 1024


def _sigmoid(x):
    return 0.5 * jnp.tanh(0.5 * x) + 0.5


def _dot(a, b):
    return jnp.dot(a, b, preferred_element_type=F32)


def _dot_nt(a, b):
    return lax.dot_general(a, b, (((1,), (1,)), ((), ())), preferred_element_type=F32)


def _dot_tn(a, b):
    return lax.dot_general(a, b, (((0,), (0,)), ((), ())), preferred_element_type=F32)


CONV_ROWS = 512
CONV_PAD = 8


def _inproj_kernel(x_ref, nw_ref, w_ref, wt_ref, wg_ref, cw_ref, cb_ref, o_ref, g_ref, xn_ref, y_ref, wb_ref,
                   *, seq):
    j = pl.program_id(1)

    @pl.when(j == 0)
    def _():
        x = x_ref[...]
        ms = jnp.mean(x * x, axis=-1, keepdims=True)
        xn = (x * lax.rsqrt(ms + NORM_EPS) * nw_ref[...]).astype(BF16)
        xn_ref[...] = xn
        g_ref[...] = _dot_nt(xn, wg_ref[...])

    is_conv = (j == COL_ML_Q) | (j == COL_ML_K)

    @pl.when(jnp.logical_not(is_conv) & (j < N_MAIN_BLOCKS))
    def _():
        o_ref[...] = _dot_nt(xn_ref[...], w_ref[...].astype(BF16)).astype(BF16)

    @pl.when(j >= N_MAIN_BLOCKS)
    def _():
        o_ref[...] = _dot_nt(xn_ref[...], wt_ref[...]).astype(BF16)

    @pl.when(is_conv)
    def _():
        which = j - COL_ML_Q
        taps = cw_ref[which]
        bias = cb_ref[pl.ds(which, 1), :]

        y_ref[0:CONV_PAD, :] = jnp.zeros((CONV_PAD, D_MODEL), F32)
        wb_ref[...] = w_ref[...].astype(BF16)
        for r in range(0, seq, CONV_ROWS):
            y_ref[CONV_PAD:, :] = _dot_nt(xn_ref[r:r + CONV_ROWS, :], wb_ref[...])
            win = y_ref[...]
            acc = bias + taps[CONV_K - 1:CONV_K, :] * win[CONV_PAD:, :]
            for back in range(1, CONV_K):
                shifted = pltpu.roll(win, back, 0)[CONV_PAD:, :]
                acc = acc + taps[CONV_K - 1 - back:CONV_K - back, :] * shifted
            o_ref[r:r + CONV_ROWS, :] = (acc * _sigmoid(acc)).astype(BF16)
            y_ref[0:CONV_PAD, :] = win[CONV_ROWS:, :]


def _inproj(x2, norm_w, w_t, layer, tail_t, gate_t, conv_w, conv_b, seq):
    m = x2.shape[0]
    n = N_COL_BLOCKS * D_MODEL
    return pl.pallas_call(
        functools.partial(_inproj_kernel, seq=seq),
        grid=(m // seq, N_COL_BLOCKS),
        in_specs=[
            pl.BlockSpec((seq, D_MODEL), lambda i, j: (i, 0)),
            pl.BlockSpec((1, D_MODEL), lambda i, j: (0, 0)),
            pl.BlockSpec((None, D_MODEL, D_MODEL), lambda i, j: (layer, jnp.minimum(j, N_MAIN_BLOCKS - 1), 0)),
            pl.BlockSpec((D_MODEL, D_MODEL), lambda i, j: (jnp.maximum(j - N_MAIN_BLOCKS, 0), 0)),
            pl.BlockSpec((2 * HEAD_W, D_MODEL), lambda i, j: (0, 0)),
            pl.BlockSpec((2, CONV_K, D_MODEL), lambda i, j: (0, 0, 0)),
            pl.BlockSpec((2, D_MODEL), lambda i, j: (0, 0)),
        ],
        out_specs=[
            pl.BlockSpec((seq, D_MODEL), lambda i, j: (i, j)),
            pl.BlockSpec((seq, 2 * HEAD_W), lambda i, j: (i, 0)),
        ],
        out_shape=[
            jax.ShapeDtypeStruct((m, n), BF16),
            jax.ShapeDtypeStruct((m, 2 * HEAD_W), F32),
        ],
        scratch_shapes=[
            pltpu.VMEM((seq, D_MODEL), BF16),
            pltpu.VMEM((CONV_ROWS + CONV_PAD, D_MODEL), F32),
            pltpu.VMEM((D_MODEL, D_MODEL), BF16),
        ],
        compiler_params=pltpu.CompilerParams(
            dimension_semantics=("arbitrary", "arbitrary"), vmem_limit_bytes=VMEM_LIMIT),
        name="inproj",
    )(x2, norm_w, w_t, tail_t, gate_t, conv_w, conv_b)


def _rel_bucket_np(rel):
    nb = N_BUCKETS // 2
    max_exact = nb // 2
    bucket = np.where(rel > 0, nb, 0)
    n = np.abs(rel)
    nf = np.maximum(n, 1).astype(np.float32)
    large = max_exact + (np.log(nf / np.float32(max_exact)) / np.float32(math.log(MAX_DISTANCE / max_exact))
                         * np.float32(nb - max_exact)).astype(np.int32)
    large = np.minimum(large, nb - 1)
    return bucket + np.where(n < max_exact, n, large)


def _bucket_tiles():
    dk = np.arange(ATT_BLOCK)[:, None]
    dq = np.arange(ATT_BLOCK)[None, :]
    diag = np.where(dk // CHUNK <= dq // CHUNK, _rel_bucket_np(dk - dq), -1)
    prev = _rel_bucket_np(dk - dq - ATT_BLOCK)
    return np.stack([diag, prev]).astype(np.int32)


FAR_BUCKET = int(_rel_bucket_np(np.array([-(ATT_BLOCK + 1)]))[0])


def _bias_kernel(rb_ref, bkt_ref, o_ref):
    h = pl.program_id(0)
    c_far = rb_ref[FAR_BUCKET, h]
    for t in range(2):
        bk = bkt_ref[t]
        acc = jnp.full((ATT_BLOCK, ATT_BLOCK), NEG_INF, F32)
        for n in range(N_BUCKETS):
            acc = jnp.where(bk == n, (rb_ref[n, h] - c_far) * LOG2E, acc)
        o_ref[0, t] = acc


def _bias_tiles(rel_bias):
    return pl.pallas_call(
        _bias_kernel,
        grid=(N_HEADS,),
        in_specs=[
            pl.BlockSpec(memory_space=pltpu.SMEM),
            pl.BlockSpec((2, ATT_BLOCK, ATT_BLOCK), lambda h: (0, 0, 0)),
        ],
        out_specs=pl.BlockSpec((1, 2, ATT_BLOCK, ATT_BLOCK), lambda h: (h, 0, 0, 0)),
        out_shape=jax.ShapeDtypeStruct((N_HEADS, 2, ATT_BLOCK, ATT_BLOCK), F32),
        name="bias_tiles",
    )(rel_bias, jnp.asarray(_bucket_tiles()))


def _attn_kernel(lam_ref, subln_ref, bias_ref, q_ref, k_ref, v_ref, z_ref, o_ref,
                 vt_ref, s_ref, *, seq, lambda_init):
    tb = ATT_BLOCK
    lamv = lam_ref[...]
    lam = (jnp.exp(jnp.sum(lamv[0:1] * lamv[1:2], axis=-1, keepdims=True))
           - jnp.exp(jnp.sum(lamv[2:3] * lamv[3:4], axis=-1, keepdims=True)) + lambda_init)
    head_cols = lambda h: slice(h * HEAD_W, (h + 1) * HEAD_W)

    for h in range(ATT_HEADS):
        for i in range(seq // HEAD_W):
            sl = slice(i * HEAD_W, (i + 1) * HEAD_W)
            vt_ref[h, 0:HEAD_W, sl] = v_ref[0, sl, head_cols(h)].astype(F32).T.astype(BF16)
        vt_ref[h, HEAD_W:, :] = jnp.ones((ONES_ROWS, seq), BF16)

    lane = lax.broadcasted_iota(jnp.int32, (tb, HEAD_W), 1)
    blocks = lambda j: slice(j * tb, (j + 1) * tb)
    units = [(h, qi, p) for qi in range(seq // tb) for h in range(ATT_HEADS) for p in range(2)]
    slot = lambda unit: units.index(unit) % (ATT_PIPE + 1)
    q_halves, col_max, normed = {}, {}, {}

    def logits(h, qi, p):
        if (h, qi) not in q_halves:
            q = (q_ref[0, blocks(qi), head_cols(h)].astype(F32) * (DA_HEAD_DIM ** -0.5 * LOG2E)).astype(BF16)
            q_halves[h, qi] = (jnp.where(lane < DA_HEAD_DIM, q, jnp.zeros_like(q)),
                               jnp.where(lane >= DA_HEAD_DIM, q, jnp.zeros_like(q)))
        s = _dot_nt(k_ref[0, 0:(qi + 1) * tb, head_cols(h)], q_halves[h, qi][p])
        for j in range(qi + 1):
            sj = s[blocks(j)]
            if j >= qi - 1:
                sj = sj + bias_ref[h, qi - j]
            s_ref[slot((h, qi, p)), blocks(j), :] = sj

    def finish_logits(h, qi, p):
        m = None
        for j in range(qi + 1):
            bm = jnp.max(s_ref[slot((h, qi, p)), blocks(j), :].reshape(tb // 8, 8, tb), axis=0)
            m = bm if m is None else jnp.maximum(m, bm)
        col_max[h, qi, p] = jnp.max(m, axis=0, keepdims=True)

    def probs(h, qi, p):
        acc = None
        for j in range(qi + 1):
            pt = jnp.exp2((s_ref[slot((h, qi, p)), blocks(j), :] - col_max[h, qi, p]).astype(BF16))
            ba = _dot(vt_ref[h, :, blocks(j)], pt)
            acc = ba if acc is None else acc + ba
        normed[h, qi, p] = acc[0:HEAD_W] * (1.0 / acc[HEAD_W:HEAD_W + 1])

    def epilogue(h, qi):
        o = normed.pop((h, qi, 0)) - lam * normed.pop((h, qi, 1))
        ms = jnp.mean(o * o, axis=0, keepdims=True)
        on = (o * lax.rsqrt(ms + SUBLN_EPS)).T
        z = z_ref[0, blocks(qi), head_cols(h)].astype(F32)
        y = on * subln_ref[...] * (1.0 - lambda_init) * (z * _sigmoid(z))
        o_ref[0, blocks(qi), head_cols(h)] = y.astype(o_ref.dtype)

    for r in range(len(units) + ATT_PIPE):
        unit = units[r] if r < len(units) else None
        late = units[r - ATT_PIPE] if r >= ATT_PIPE else None
        if unit is not None:
            logits(*unit)
        if late is not None:
            probs(*late)
        if unit is not None:
            finish_logits(*unit)
        if late is not None and late[2] == 1:
            epilogue(late[0], late[1])


def _attention(proj3, bias_tiles, lam, subln_w, lambda_init):
    b, s, _ = proj3.shape
    width = ATT_HEADS * HEAD_W

    def col(block):
        return pl.BlockSpec((1, s, width), lambda bi, hi: (bi, 0, block * (N_HEADS // ATT_HEADS) + hi))

    return pl.pallas_call(
        functools.partial(_attn_kernel, seq=s, lambda_init=lambda_init),
        grid=(b, N_HEADS // ATT_HEADS),
        in_specs=[
            pl.BlockSpec((4, DA_HEAD_DIM), lambda bi, hi: (0, 0)),
            pl.BlockSpec((1, HEAD_W), lambda bi, hi: (0, 0)),
            pl.BlockSpec((ATT_HEADS, 2, ATT_BLOCK, ATT_BLOCK), lambda bi, hi: (hi, 0, 0, 0)),
            col(COL_DA_Q), col(COL_DA_K), col(COL_DA_V), col(COL_DA_Z),
        ],
        out_specs=pl.BlockSpec((1, s, width), lambda bi, hi: (bi, 0, hi)),
        out_shape=jax.ShapeDtypeStruct((b, s, D_MODEL), BF16),
        scratch_shapes=[
            pltpu.VMEM((ATT_HEADS, HEAD_W + ONES_ROWS, s), BF16),
            pltpu.VMEM((ATT_PIPE + 1, s, ATT_BLOCK), F32),
        ],
        compiler_params=pltpu.CompilerParams(
            dimension_semantics=("arbitrary", "arbitrary"), vmem_limit_bytes=VMEM_LIMIT),
        name="diff_attention",
    )(lam, subln_w, bias_tiles, proj3, proj3, proj3, proj3)


_BCAST_PARTS = (3, 3, 2)
_DECAY_PARTS = 2


_BCAST_W = len(_BCAST_PARTS) * HEAD_W


def _bcast_select_np():
    e = np.zeros((HEAD_W, N_HEADS * _BCAST_W), np.float32)
    g = 0
    for o, parts in enumerate(_BCAST_PARTS):
        for _ in range(parts):
            for h in range(N_HEADS):
                e[8 * g + h, h * _BCAST_W + o * HEAD_W:h * _BCAST_W + (o + 1) * HEAD_W] = 1.0
            g += 1
    return e


def _row_select_np():
    e = np.zeros((HEAD_W, N_HEADS * HEAD_W), np.float32)
    for g in range(_DECAY_PARTS):
        for h in range(N_HEADS):
            e[8 * g + h, h * HEAD_W:(h + 1) * HEAD_W] = 1.0
    return e


def _split_parts(x, n):
    parts = []
    r = x
    for i in range(n):
        p = r.astype(BF16).astype(F32)
        parts.append(p)
        if i + 1 < n:
            r = r - p
    return parts


def _pack_groups(arrays_and_parts, group):
    parts = []
    for x, n in arrays_and_parts:
        parts.extend(_split_parts(x, n))
    packed = jnp.zeros_like(parts[0])
    for gi, p in enumerate(parts):
        packed = jnp.where(group == gi, p, packed)
    return packed.astype(BF16)


def _mlstm_kernel(q_ref, k_ref, v_ref, g_ref, bif_ref, mhw_ref, ltri_ref, esel_ref, rsel_ref, y_ref,
                  bc_ref, wpt_ref, wst_ref, dcrow_ref, c_ref, m_ref, *, tblk):
    L = ML_CHUNK
    n_chunks = tblk // L

    @pl.when(pl.program_id(1) == 0)
    def _():
        c_ref[...] = jnp.zeros_like(c_ref)
        m_ref[...] = jnp.zeros_like(m_ref)

    row = lax.broadcasted_iota(jnp.int32, (tblk, HEAD_W), 0) % L
    group = lax.broadcasted_iota(jnp.int32, (tblk, HEAD_W), 1) // 8
    ig = g_ref[0, :, 0:HEAD_W] + bif_ref[0:1, :]
    fg = g_ref[0, :, HEAD_W:2 * HEAD_W] + bif_ref[1:2, :]
    lf = jnp.minimum(fg, 0.0) - jnp.log(1.0 + jnp.exp(-jnp.abs(fg)))
    ltri = ltri_ref[...]
    lf_parts = [p.astype(BF16) for p in _split_parts(lf, 3)]
    bcum = jnp.concatenate(
        [sum(_dot(ltri, p[c * L:(c + 1) * L]) for p in lf_parts) for c in range(n_chunks)], axis=0)
    wp = ig - bcum
    cmax = wp
    sh = 1
    while sh < L:
        cmax = jnp.where(row >= sh, jnp.maximum(cmax, pltpu.roll(cmax, sh, 0)), cmax)
        sh *= 2
    m_rows, b_last_rows, m_next_rows = [], [], []
    m_state = m_ref[...]
    for c in range(n_chunks):
        last = (c + 1) * L - 1
        b_last = bcum[last:last + 1, :]
        m_next = b_last + jnp.maximum(m_state, cmax[last:last + 1, :])
        dcrow_ref[c] = _dot(
            _pack_groups([(jnp.broadcast_to(jnp.exp(b_last + m_state - m_next), (8, HEAD_W)), _DECAY_PARTS)],
                         group[0:8]), rsel_ref[...])
        m_rows.append(jnp.broadcast_to(m_state, (L, HEAD_W)))
        b_last_rows.append(jnp.broadcast_to(b_last, (L, HEAD_W)))
        m_next_rows.append(jnp.broadcast_to(m_next, (L, HEAD_W)))
        m_state = m_next
    m_ref[...] = m_state
    m_prev = jnp.concatenate(m_rows, axis=0)
    inter = bcum + m_prev
    m_t = jnp.maximum(inter, bcum + cmax)
    w_inter = jnp.exp(inter - m_t)
    w_state = jnp.exp(jnp.concatenate(b_last_rows, axis=0) + wp - jnp.concatenate(m_next_rows, axis=0))
    k_scale = HEAD_W ** -0.5
    bc_ref[...] = _dot(_pack_groups(
        [((bcum - m_t) * LOG2E + math.log2(k_scale), 3), (m_t * -LOG2E, 3), (w_inter * k_scale, 2)], group),
        esel_ref[...])
    wp2 = wp * LOG2E
    for c in range(n_chunks):
        wpt_ref[c] = wp2[c * L:(c + 1) * L].T
        wst_ref[c] = w_state[c * L:(c + 1) * L].T

    causal = (lax.broadcasted_iota(jnp.int32, (L, HEAD_W), 1)
              <= lax.broadcasted_iota(jnp.int32, (L, HEAD_W), 0))
    ones_blk = jnp.ones((L, HEAD_W), BF16)

    items = [dict(c=c, h=h, rows=slice(c * L, (c + 1) * L), hs=slice(h * HEAD_W, (h + 1) * HEAD_W))
             for c in range(n_chunks) for h in range(N_HEADS)]

    def bcast(st, o):
        h = st["h"]
        return bc_ref[st["rows"], h * _BCAST_W + o * HEAD_W:h * _BCAST_W + (o + 1) * HEAD_W]

    def stage_scores(st):
        h = st["h"]
        st["q"], st["k"], st["v"] = (ref[0, st["rows"], st["hs"]] for ref in (q_ref, k_ref, v_ref))
        st["s"] = _dot_nt(st["q"], st["k"])
        k_t = (st["k"].astype(F32).T * wst_ref[st["c"], h:h + 1, :]).astype(BF16)
        st["upd"] = _dot(k_t, jnp.concatenate([st["v"], ones_blk], axis=1))

    def stage_readout(st):
        h = st["h"]
        w_intra = jnp.exp2(jnp.where(causal, bcast(st, 0) + wpt_ref[st["c"], h:h + 1, :], NEG_INF))
        scores = (st["s"] * w_intra).astype(BF16)
        lhs = jnp.concatenate([scores, (st["q"].astype(F32) * bcast(st, 2)).astype(BF16)], axis=1)
        st["c_old"] = c_ref[h]
        rhs = jnp.concatenate(
            [jnp.concatenate([st["v"], ones_blk], axis=1), st["c_old"].astype(BF16)], axis=0)
        st["tot"] = _dot(lhs, rhs)

    def stage_finish(st):
        tot = st["tot"]
        hh = tot[:, 0:HEAD_W] / jnp.maximum(jnp.abs(tot[:, HEAD_W:]), jnp.exp2(bcast(st, 1)))
        mu = jnp.mean(hh, axis=-1, keepdims=True)
        hc = hh - mu
        var = jnp.mean(hc * hc, axis=-1, keepdims=True)
        y_ref[0, st["rows"], st["hs"]] = (
            hc * lax.rsqrt(var + HEAD_LN_EPS) * mhw_ref[:, st["hs"]]).astype(y_ref.dtype)
        dc = dcrow_ref[st["c"], 0:1, st["hs"]]
        c_ref[st["h"]] = jnp.concatenate([dc, dc], axis=1) * st["c_old"] + st["upd"]
        st.clear()

    assert 2 * HEAD_PIPE < N_HEADS
    for i in range(len(items) + 2 * HEAD_PIPE):
        if i < len(items):
            stage_scores(items[i])
        if 0 <= i - HEAD_PIPE < len(items):
            stage_readout(items[i - HEAD_PIPE])
        if 0 <= i - 2 * HEAD_PIPE < len(items):
            stage_finish(items[i - 2 * HEAD_PIPE])


def _mlstm(proj3, gates3, bif, mh_w, tblk):
    b, s, _ = proj3.shape
    n_chunks = tblk // ML_CHUNK

    def col(block):
        return pl.BlockSpec((1, tblk, D_MODEL), lambda bi, ti: (bi, ti, block))

    def full(shape):
        return pl.BlockSpec(shape, lambda bi, ti: (0,) * len(shape))

    ltri = jnp.asarray(np.tril(np.ones((ML_CHUNK, ML_CHUNK), np.float32)), BF16)
    esel = jnp.asarray(_bcast_select_np(), BF16)
    rsel = jnp.asarray(_row_select_np(), BF16)
    return pl.pallas_call(
        functools.partial(_mlstm_kernel, tblk=tblk),
        grid=(b, s // tblk),
        in_specs=[
            col(COL_ML_Q), col(COL_ML_K), col(COL_ML_V),
            pl.BlockSpec((1, tblk, 2 * HEAD_W), lambda bi, ti: (bi, ti, 0)),
            full((2, HEAD_W)), full((1, D_MODEL)),
            full((ML_CHUNK, ML_CHUNK)), full(esel.shape), full(rsel.shape),
        ],
        out_specs=pl.BlockSpec((1, tblk, D_MODEL), lambda bi, ti: (bi, ti, 0)),
        out_shape=jax.ShapeDtypeStruct((b, s, D_MODEL), BF16),
        scratch_shapes=[
            pltpu.VMEM((tblk, N_HEADS * _BCAST_W), F32),
            pltpu.VMEM((n_chunks, HEAD_W, ML_CHUNK), F32),
            pltpu.VMEM((n_chunks, HEAD_W, ML_CHUNK), F32),
            pltpu.VMEM((n_chunks, 8, D_MODEL), F32),
            pltpu.VMEM((N_HEADS, HEAD_W, 2 * HEAD_W), F32),
            pltpu.VMEM((1, HEAD_W), F32),
        ],
        compiler_params=pltpu.CompilerParams(
            dimension_semantics=("arbitrary", "arbitrary"), vmem_limit_bytes=VMEM_LIMIT),
        name="mlstm",
    )(proj3, proj3, proj3, gates3, bif, mh_w, ltri, esel, rsel)


def _out_kernel(ya_ref, hn_ref, og_ref, zg_ref, ga_ref, gm_ref, x_ref, wpa_ref, wpm_ref, wout_ref, bg_ref,
                nf_ref, o_ref, *, final_norm):
    branches = {}

    def project_branches(r):
        rows = slice(r, r + OUT_ROWS)
        zg = zg_ref[rows, :].astype(F32)
        y_m = ((_sigmoid(og_ref[rows, :].astype(F32)) * hn_ref[rows, :].astype(F32))
               * (zg * _sigmoid(zg))).astype(BF16)
        branches[r] = (_dot(ya_ref[rows, :], wpa_ref[...]), _dot(y_m, wpm_ref[...]))

    def merge_and_project(r):
        rows = slice(r, r + OUT_ROWS)
        a, b = branches.pop(r)
        g_a = _sigmoid(ga_ref[rows, :].astype(F32) + bg_ref[0:1, :])
        g_m = _sigmoid(gm_ref[rows, :].astype(F32) + bg_ref[1:2, :])
        hres = x_ref[rows, :] + _dot((g_a * a + g_m * b).astype(BF16), wout_ref[...])
        if final_norm:
            ms = jnp.mean(hres * hres, axis=-1, keepdims=True)
            hres = hres * lax.rsqrt(ms + NORM_EPS) * nf_ref[...]
        o_ref[rows, :] = hres

    tm = o_ref.shape[0]
    project_branches(0)
    for r in range(0, tm, OUT_ROWS):
        if r + OUT_ROWS < tm:
            project_branches(r + OUT_ROWS)
        merge_and_project(r)


def _out_stage(y_a, h_n, proj, x2, w_pa, w_pm, w_out, b_gate, norm_final, tm, final_norm):
    m = x2.shape[0]

    def rows(block=0):
        return pl.BlockSpec((tm, D_MODEL), lambda i: (i, block))

    def full(shape):
        return pl.BlockSpec(shape, lambda i: (0,) * len(shape))

    return pl.pallas_call(
        functools.partial(_out_kernel, final_norm=final_norm),
        grid=(m // tm,),
        in_specs=[
            rows(), rows(), rows(COL_ML_O), rows(COL_ML_Z), rows(COL_GATE_A), rows(COL_GATE_M), rows(),
            full((D_MODEL, D_MODEL)), full((D_MODEL, D_MODEL)), full((D_MODEL, D_MODEL)),
            full((2, D_MODEL)), full((1, D_MODEL)),
        ],
        out_specs=rows(),
        out_shape=jax.ShapeDtypeStruct((m, D_MODEL), F32),
        compiler_params=pltpu.CompilerParams(
            dimension_semantics=("arbitrary",), vmem_limit_bytes=VMEM_LIMIT),
        name="out_stage",
    )(y_a, h_n, proj, proj, proj, proj, x2, w_pa, w_pm, w_out, b_gate, norm_final)


def _transposed_in_weight(w_in, layer):
    w_t = jnp.swapaxes(w_in, 1, 2)
    n_wide = N_MAIN_BLOCKS * D_MODEL
    tail_t = w_t[layer, n_wide + 2 * N_HEADS:, :].astype(BF16)
    reps = HEAD_W // N_HEADS
    gate_t = jnp.concatenate([jnp.tile(w_t[layer, n_wide:n_wide + N_HEADS, :], (reps, 1)),
                              jnp.tile(w_t[layer, n_wide + N_HEADS:n_wide + 2 * N_HEADS, :], (reps, 1))],
                             axis=0).astype(BF16)
    return w_t, tail_t, gate_t


def kernel(x, norm_w, w_in, lam, subln_w, rel_bias, conv_w, conv_b, b_if, mh_w, b_gate, w_pa, w_pm, w_out,
           norm_final):
    b, s, d = x.shape
    depth = norm_w.shape[0]
    m = b * s
    tm_out = min(1024, m)
    tblk = min(1024, s)
    bias_tiles = _bias_tiles(rel_bias)
    h2 = x.reshape(m, d)
    for layer in range(depth):
        lambda_init = 0.8 - 0.6 * math.exp(-0.3 * layer)
        w_t, tail_t, gate_t = _transposed_in_weight(w_in, layer)
        proj, gates = _inproj(h2, norm_w[layer][None, :], w_t, layer, tail_t, gate_t,
                              conv_w[layer], conv_b[layer], s)
        proj3 = proj.reshape(b, s, N_COL_BLOCKS * D_MODEL)
        y_a = _attention(proj3, bias_tiles, lam[layer], subln_w[layer][None, :], lambda_init)
        bif = jnp.tile(b_if[layer], (1, HEAD_W // N_HEADS))
        h_n = _mlstm(proj3, gates.reshape(b, s, 2 * HEAD_W), bif, mh_w[layer][None, :], tblk)
        h2 = _out_stage(y_a.reshape(m, d), h_n.reshape(m, d), proj, h2,
                        w_pa[layer].astype(BF16), w_pm[layer].astype(BF16), w_out[layer].astype(BF16),
                        b_gate[layer], norm_final[None, :], tm_out, final_norm=(layer == depth - 1))
    return h2.reshape(b, s, d)
```

```python
import functools
import math

import numpy as np
import jax
import jax.numpy as jnp
from jax import lax
from jax.experimental import pallas as pl
from jax.experimental.pallas import tpu as pltpu

F32 = jnp.float32
BF16 = jnp.bfloat16

D_MODEL = 1024
N_HEADS = 8
HEAD_W = 128
DA_HEAD_DIM = 64
CHUNK = 64
CONV_K = 4
N_BUCKETS = 32
MAX_DISTANCE = 128
NORM_EPS = 1e-6
SUBLN_EPS = 1e-5
HEAD_LN_EPS = 1e-5
NEG_INF = -1e30
LOG2E = 1.4426950408889634

COL_DA_Q, COL_DA_K, COL_DA_V, COL_DA_Z = 0, 1, 2, 3
COL_ML_Q, COL_ML_K, COL_ML_V, COL_ML_O, COL_ML_Z = 4, 5, 6, 7, 8
COL_GATE_A, COL_GATE_M = 9, 10
N_COL_BLOCKS = 11
N_MAIN_BLOCKS = 9

ATT_BLOCK = 256
ONES_ROWS = 16
ATT_PIPE = 2
ATT_HEADS = 2
ML_CHUNK = 128
HEAD_PIPE = 1
OUT_ROWS = 256
VMEM_LIMIT = 56 * 1024 * 1024


def _sigmoid(x):
    return 0.5 * jnp.tanh(0.5 * x) + 0.5


def _dot(a, b):
    return jnp.dot(a, b, preferred_element_type=F32)


def _dot_nt(a, b):
    return lax.dot_general(a, b, (((1,), (1,)), ((), ())), preferred_element_type=F32)


CONV_ROWS = 256
CONV_PAD = 8


def _inproj_kernel(x_ref, nw_ref, w_ref, wt_ref, wg_ref, cw_ref, cb_ref, o_ref, g_ref, xn_ref, y_ref, wb_ref,
                   *, seq):
    j = pl.program_id(1)

    @pl.when(j == 0)
    def _():
        x = x_ref[...]
        ms = jnp.mean(x * x, axis=-1, keepdims=True)
        xn = (x * lax.rsqrt(ms + NORM_EPS) * nw_ref[...]).astype(BF16)
        xn_ref[...] = xn
        g_ref[...] = _dot_nt(xn, wg_ref[...])

    is_conv = (j == COL_ML_Q) | (j == COL_ML_K)

    @pl.when(jnp.logical_not(is_conv) & (j < N_MAIN_BLOCKS))
    def _():
        o_ref[...] = _dot_nt(xn_ref[...], w_ref[...].astype(BF16)).astype(BF16)

    @pl.when(j >= N_MAIN_BLOCKS)
    def _():
        o_ref[...] = _dot_nt(xn_ref[...], wt_ref[...]).astype(BF16)

    @pl.when(is_conv)
    def _():
        which = j - COL_ML_Q
        taps = cw_ref[which]
        bias = cb_ref[pl.ds(which, 1), :]

        y_ref[0:CONV_PAD, :] = jnp.zeros((CONV_PAD, D_MODEL), F32)
        wb_ref[...] = w_ref[...].astype(BF16)
        for r in range(0, seq, CONV_ROWS):
            y_ref[CONV_PAD:, :] = _dot_nt(xn_ref[r:r + CONV_ROWS, :], wb_ref[...])
            win = y_ref[...]
            acc = bias + taps[CONV_K - 1:CONV_K, :] * win[CONV_PAD:, :]
            for back in range(1, CONV_K):
                shifted = pltpu.roll(win, back, 0)[CONV_PAD:, :]
                acc = acc + taps[CONV_K - 1 - back:CONV_K - back, :] * shifted
            o_ref[r:r + CONV_ROWS, :] = (acc * _sigmoid(acc)).astype(BF16)
            y_ref[0:CONV_PAD, :] = win[CONV_ROWS:, :]


def _inproj(x2, norm_w, w_t, layer, tail_t, gate_t, conv_w, conv_b, seq):
    m = x2.shape[0]
    n = N_COL_BLOCKS * D_MODEL
    return pl.pallas_call(
        functools.partial(_inproj_kernel, seq=seq),
        grid=(m // seq, N_COL_BLOCKS),
        in_specs=[
            pl.BlockSpec((seq, D_MODEL), lambda i, j: (i, 0)),
            pl.BlockSpec((1, D_MODEL), lambda i, j: (0, 0)),
            pl.BlockSpec((None, D_MODEL, D_MODEL), lambda i, j: (layer, jnp.minimum(j, N_MAIN_BLOCKS - 1), 0)),
            pl.BlockSpec((D_MODEL, D_MODEL), lambda i, j: (jnp.maximum(j - N_MAIN_BLOCKS, 0), 0)),
            pl.BlockSpec((2 * HEAD_W, D_MODEL), lambda i, j: (0, 0)),
            pl.BlockSpec((2, CONV_K, D_MODEL), lambda i, j: (0, 0, 0)),
            pl.BlockSpec((2, D_MODEL), lambda i, j: (0, 0)),
        ],
        out_specs=[
            pl.BlockSpec((seq, D_MODEL), lambda i, j: (i, j)),
            pl.BlockSpec((seq, 2 * HEAD_W), lambda i, j: (i, 0)),
        ],
        out_shape=[
            jax.ShapeDtypeStruct((m, n), BF16),
            jax.ShapeDtypeStruct((m, 2 * HEAD_W), F32),
        ],
        scratch_shapes=[
            pltpu.VMEM((seq, D_MODEL), BF16),
            pltpu.VMEM((CONV_ROWS + CONV_PAD, D_MODEL), F32),
            pltpu.VMEM((D_MODEL, D_MODEL), BF16),
        ],
        compiler_params=pltpu.CompilerParams(
            dimension_semantics=("arbitrary", "arbitrary"), vmem_limit_bytes=VMEM_LIMIT),
        name="inproj",
    )(x2, norm_w, w_t, tail_t, gate_t, conv_w, conv_b)


def _rel_bucket_np(rel):
    nb = N_BUCKETS // 2
    max_exact = nb // 2
    bucket = np.where(rel > 0, nb, 0)
    n = np.abs(rel)
    nf = np.maximum(n, 1).astype(np.float32)
    large = max_exact + (np.log(nf / np.float32(max_exact)) / np.float32(math.log(MAX_DISTANCE / max_exact))
                         * np.float32(nb - max_exact)).astype(np.int32)
    large = np.minimum(large, nb - 1)
    return bucket + np.where(n < max_exact, n, large)


def _bucket_tiles():
    dk = np.arange(ATT_BLOCK)[:, None]
    dq = np.arange(ATT_BLOCK)[None, :]
    diag = np.where(dk // CHUNK <= dq // CHUNK, _rel_bucket_np(dk - dq), -1)
    prev = _rel_bucket_np(dk - dq - ATT_BLOCK)
    return np.stack([diag, prev]).astype(np.int32)


FAR_BUCKET = int(_rel_bucket_np(np.array([-(ATT_BLOCK + 1)]))[0])


def _bias_kernel(rb_ref, bkt_ref, o_ref):
    h = pl.program_id(0)
    c_far = rb_ref[FAR_BUCKET, h]
    for t in range(2):
        bk = bkt_ref[t]
        acc = jnp.full((ATT_BLOCK, ATT_BLOCK), NEG_INF, F32)
        for n in range(N_BUCKETS):
            acc = jnp.where(bk == n, (rb_ref[n, h] - c_far) * LOG2E, acc)
        o_ref[0, t] = acc


def _bias_tiles(rel_bias):
    return pl.pallas_call(
        _bias_kernel,
        grid=(N_HEADS,),
        in_specs=[
            pl.BlockSpec(memory_space=pltpu.SMEM),
            pl.BlockSpec((2, ATT_BLOCK, ATT_BLOCK), lambda h: (0, 0, 0)),
        ],
        out_specs=pl.BlockSpec((1, 2, ATT_BLOCK, ATT_BLOCK), lambda h: (h, 0, 0, 0)),
        out_shape=jax.ShapeDtypeStruct((N_HEADS, 2, ATT_BLOCK, ATT_BLOCK), F32),
        name="bias_tiles",
    )(rel_bias, jnp.asarray(_bucket_tiles()))


def _attn_kernel(lam_ref, subln_ref, bias_ref, q_ref, k_ref, v_ref, z_ref, o_ref,
                 vt_ref, s_ref, *, seq, lambda_init):
    tb = ATT_BLOCK
    lamv = lam_ref[...]
    lam = (jnp.exp(jnp.sum(lamv[0:1] * lamv[1:2], axis=-1, keepdims=True))
           - jnp.exp(jnp.sum(lamv[2:3] * lamv[3:4], axis=-1, keepdims=True)) + lambda_init)
    head_cols = lambda h: slice(h * HEAD_W, (h + 1) * HEAD_W)

    for h in range(ATT_HEADS):
        for i in range(seq // HEAD_W):
            sl = slice(i * HEAD_W, (i + 1) * HEAD_W)
            vt_ref[h, 0:HEAD_W, sl] = v_ref[0, sl, head_cols(h)].astype(F32).T.astype(BF16)
        vt_ref[h, HEAD_W:, :] = jnp.ones((ONES_ROWS, seq), BF16)

    lane = lax.broadcasted_iota(jnp.int32, (tb, HEAD_W), 1)
    blocks = lambda j: slice(j * tb, (j + 1) * tb)
    units = [(h, qi, p) for qi in range(seq // tb) for h in range(ATT_HEADS) for p in range(2)]
    slot = lambda unit: units.index(unit) % (ATT_PIPE + 1)
    q_halves, col_max, normed = {}, {}, {}

    def logits(h, qi, p):
        if (h, qi) not in q_halves:
            q = (q_ref[0, blocks(qi), head_cols(h)].astype(F32) * (DA_HEAD_DIM ** -0.5 * LOG2E)).astype(BF16)
            q_halves[h, qi] = (jnp.where(lane < DA_HEAD_DIM, q, jnp.zeros_like(q)),
                               jnp.where(lane >= DA_HEAD_DIM, q, jnp.zeros_like(q)))
        s = _dot_nt(k_ref[0, 0:(qi + 1) * tb, head_cols(h)], q_halves[h, qi][p])
        for j in range(qi + 1):
            sj = s[blocks(j)]
            if j >= qi - 1:
                sj = sj + bias_ref[h, qi - j]
            s_ref[slot((h, qi, p)), blocks(j), :] = sj

    def finish_logits(h, qi, p):
        m = None
        for j in range(qi + 1):
            bm = jnp.max(s_ref[slot((h, qi, p)), blocks(j), :].reshape(tb // 8, 8, tb), axis=0)
            m = bm if m is None else jnp.maximum(m, bm)
        col_max[h, qi, p] = jnp.max(m, axis=0, keepdims=True)

    def probs(h, qi, p):
        acc = None
        m = col_max[h, qi, p]
        half = tb // 2
        assert half % CHUNK == 0
        for j in range(qi + 1):
            if j < qi:
                pt = jnp.exp2((s_ref[slot((h, qi, p)), blocks(j), :] - m).astype(BF16))
            else:
                top = jnp.exp2((s_ref[slot((h, qi, p)), j * tb:j * tb + half, :] - m).astype(BF16))
                right = jnp.exp2((s_ref[slot((h, qi, p)), j * tb + half:(j + 1) * tb, half:]
                                  - m[:, half:]).astype(BF16))
                pt = jnp.concatenate(
                    [top, jnp.concatenate([jnp.zeros((half, half), BF16), right], axis=1)], axis=0)
            ba = _dot(vt_ref[h, :, blocks(j)], pt)
            acc = ba if acc is None else acc + ba
        normed[h, qi, p] = acc[0:HEAD_W] * (1.0 / acc[HEAD_W:HEAD_W + 1])

    def epilogue(h, qi):
        o = normed.pop((h, qi, 0)) - lam * normed.pop((h, qi, 1))
        ms = jnp.mean(o * o, axis=0, keepdims=True)
        on = (o * lax.rsqrt(ms + SUBLN_EPS)).T
        z = z_ref[0, blocks(qi), head_cols(h)].astype(F32)
        y = on * subln_ref[...] * (1.0 - lambda_init) * (z * _sigmoid(z))
        o_ref[0, blocks(qi), head_cols(h)] = y.astype(o_ref.dtype)

    for r in range(len(units) + ATT_PIPE):
        unit = units[r] if r < len(units) else None
        late = units[r - ATT_PIPE] if r >= ATT_PIPE else None
        if unit is not None:
            logits(*unit)
        if late is not None:
            probs(*late)
        if unit is not None:
            finish_logits(*unit)
        if late is not None and late[2] == 1:
            epilogue(late[0], late[1])


def _attention(proj3, bias_tiles, lam, subln_w, lambda_init):
    b, s, _ = proj3.shape
    width = ATT_HEADS * HEAD_W

    def col(block):
        return pl.BlockSpec((1, s, width), lambda bi, hi: (bi, 0, block * (N_HEADS // ATT_HEADS) + hi))

    return pl.pallas_call(
        functools.partial(_attn_kernel, seq=s, lambda_init=lambda_init),
        grid=(b, N_HEADS // ATT_HEADS),
        in_specs=[
            pl.BlockSpec((4, DA_HEAD_DIM), lambda bi, hi: (0, 0)),
            pl.BlockSpec((1, HEAD_W), lambda bi, hi: (0, 0)),
            pl.BlockSpec((ATT_HEADS, 2, ATT_BLOCK, ATT_BLOCK), lambda bi, hi: (hi, 0, 0, 0)),
            col(COL_DA_Q), col(COL_DA_K), col(COL_DA_V), col(COL_DA_Z),
        ],
        out_specs=pl.BlockSpec((1, s, width), lambda bi, hi: (bi, 0, hi)),
        out_shape=jax.ShapeDtypeStruct((b, s, D_MODEL), BF16),
        scratch_shapes=[
            pltpu.VMEM((ATT_HEADS, HEAD_W + ONES_ROWS, s), BF16),
            pltpu.VMEM((ATT_PIPE + 1, s, ATT_BLOCK), F32),
        ],
        compiler_params=pltpu.CompilerParams(
            dimension_semantics=("arbitrary", "arbitrary"), vmem_limit_bytes=VMEM_LIMIT),
        name="diff_attention",
    )(lam, subln_w, bias_tiles, proj3, proj3, proj3, proj3)


_BCAST_PARTS = (3, 3, 2)
_DECAY_PARTS = 2


_BCAST_W = len(_BCAST_PARTS) * HEAD_W


def _bcast_select_np():
    e = np.zeros((HEAD_W, N_HEADS * _BCAST_W), np.float32)
    g = 0
    for o, parts in enumerate(_BCAST_PARTS):
        for _ in range(parts):
            for h in range(N_HEADS):
                e[8 * g + h, h * _BCAST_W + o * HEAD_W:h * _BCAST_W + (o + 1) * HEAD_W] = 1.0
            g += 1
    return e


def _row_select_np():
    e = np.zeros((HEAD_W, N_HEADS * HEAD_W), np.float32)
    for g in range(_DECAY_PARTS):
        for h in range(N_HEADS):
            e[8 * g + h, h * HEAD_W:(h + 1) * HEAD_W] = 1.0
    return e


def _split_parts(x, n):
    parts = []
    r = x
    for i in range(n):
        p = r.astype(BF16).astype(F32)
        parts.append(p)
        if i + 1 < n:
            r = r - p
    return parts


def _pack_groups(arrays_and_parts, group):
    parts = []
    for x, n in arrays_and_parts:
        parts.extend(_split_parts(x, n))
    packed = jnp.zeros_like(parts[0])
    for gi, p in enumerate(parts):
        packed = jnp.where(group == gi, p, packed)
    return packed.astype(BF16)


def _mlstm_kernel(q_ref, k_ref, v_ref, g_ref, bif_ref, mhw_ref, ltri_ref, esel_ref, rsel_ref, y_ref,
                  bc_ref, wpt_ref, wst_ref, dcrow_ref, c_ref, m_ref, *, tblk):
    L = ML_CHUNK
    n_chunks = tblk // L

    @pl.when(pl.program_id(1) == 0)
    def _():
        c_ref[...] = jnp.zeros_like(c_ref)
        m_ref[...] = jnp.zeros_like(m_ref)

    row = lax.broadcasted_iota(jnp.int32, (tblk, HEAD_W), 0) % L
    group = lax.broadcasted_iota(jnp.int32, (tblk, HEAD_W), 1) // 8
    ig = g_ref[0, :, 0:HEAD_W] + bif_ref[0:1, :]
    fg = g_ref[0, :, HEAD_W:2 * HEAD_W] + bif_ref[1:2, :]
    lf = jnp.minimum(fg, 0.0) - jnp.log(1.0 + jnp.exp(-jnp.abs(fg)))
    ltri = ltri_ref[...]
    lf_parts = [p.astype(BF16) for p in _split_parts(lf, 3)]
    bcum = jnp.concatenate(
        [sum(_dot(ltri, p[c * L:(c + 1) * L]) for p in lf_parts) for c in range(n_chunks)], axis=0)
    wp = ig - bcum
    cmax = wp
    sh = 1
    while sh < L:
        cmax = jnp.where(row >= sh, jnp.maximum(cmax, pltpu.roll(cmax, sh, 0)), cmax)
        sh *= 2
    m_rows, b_last_rows, m_next_rows = [], [], []
    m_state = m_ref[...]
    for c in range(n_chunks):
        last = (c + 1) * L - 1
        b_last = bcum[last:last + 1, :]
        m_next = b_last + jnp.maximum(m_state, cmax[last:last + 1, :])
        dcrow_ref[c] = _dot(
            _pack_groups([(jnp.broadcast_to(jnp.exp(b_last + m_state - m_next), (8, HEAD_W)), _DECAY_PARTS)],
                         group[0:8]), rsel_ref[...])
        m_rows.append(jnp.broadcast_to(m_state, (L, HEAD_W)))
        b_last_rows.append(jnp.broadcast_to(b_last, (L, HEAD_W)))
        m_next_rows.append(jnp.broadcast_to(m_next, (L, HEAD_W)))
        m_state = m_next
    m_ref[...] = m_state
    m_prev = jnp.concatenate(m_rows, axis=0)
    inter = bcum + m_prev
    m_t = jnp.maximum(inter, bcum + cmax)
    w_inter = jnp.exp(inter - m_t)
    w_state = jnp.exp(jnp.concatenate(b_last_rows, axis=0) + wp - jnp.concatenate(m_next_rows, axis=0))
    k_scale = HEAD_W ** -0.5
    bc_ref[...] = _dot(_pack_groups(
        [((bcum - m_t) * LOG2E + math.log2(k_scale), 3), (m_t * -LOG2E, 3), (w_inter * k_scale, 2)], group),
        esel_ref[...])
    wp2 = wp * LOG2E
    for c in range(n_chunks):
        wpt_ref[c] = wp2[c * L:(c + 1) * L].T
        wst_ref[c] = w_state[c * L:(c + 1) * L].T

    causal = (lax.broadcasted_iota(jnp.int32, (L, HEAD_W), 1)
              <= lax.broadcasted_iota(jnp.int32, (L, HEAD_W), 0))
    ones_blk = jnp.ones((L, HEAD_W), BF16)

    items = [dict(c=c, h=h, rows=slice(c * L, (c + 1) * L), hs=slice(h * HEAD_W, (h + 1) * HEAD_W))
             for c in range(n_chunks) for h in range(N_HEADS)]

    def bcast(st, o):
        h = st["h"]
        return bc_ref[st["rows"], h * _BCAST_W + o * HEAD_W:h * _BCAST_W + (o + 1) * HEAD_W]

    def stage_scores(st):
        h = st["h"]
        st["q"], st["k"], st["v"] = (ref[0, st["rows"], st["hs"]] for ref in (q_ref, k_ref, v_ref))
        st["s"] = _dot_nt(st["q"], st["k"])
        k_t = (st["k"].astype(F32).T * wst_ref[st["c"], h:h + 1, :]).astype(BF16)
        st["upd"] = _dot(k_t, jnp.concatenate([st["v"], ones_blk], axis=1))

    def stage_readout(st):
        h = st["h"]
        w_intra = jnp.exp2(jnp.where(causal, bcast(st, 0) + wpt_ref[st["c"], h:h + 1, :], NEG_INF))
        scores = (st["s"] * w_intra).astype(BF16)
        lhs = jnp.concatenate([scores, (st["q"].astype(F32) * bcast(st, 2)).astype(BF16)], axis=1)
        st["c_old"] = c_ref[h]
        rhs = jnp.concatenate(
            [jnp.concatenate([st["v"], ones_blk], axis=1), st["c_old"].astype(BF16)], axis=0)
        st["tot"] = _dot(lhs, rhs)

    def stage_finish(st):
        tot = st["tot"]
        hh = tot[:, 0:HEAD_W] / jnp.maximum(jnp.abs(tot[:, HEAD_W:]), jnp.exp2(bcast(st, 1)))
        mu = jnp.mean(hh, axis=-1, keepdims=True)
        hc = hh - mu
        var = jnp.mean(hc * hc, axis=-1, keepdims=True)
        y_ref[0, st["rows"], st["hs"]] = (
            hc * lax.rsqrt(var + HEAD_LN_EPS) * mhw_ref[:, st["hs"]]).astype(y_ref.dtype)
        dc = dcrow_ref[st["c"], 0:1, st["hs"]]
        c_ref[st["h"]] = jnp.concatenate([dc, dc], axis=1) * st["c_old"] + st["upd"]
        st.clear()

    assert 2 * HEAD_PIPE < N_HEADS
    for i in range(len(items) + 2 * HEAD_PIPE):
        if i < len(items):
            stage_scores(items[i])
        if 0 <= i - HEAD_PIPE < len(items):
            stage_readout(items[i - HEAD_PIPE])
        if 0 <= i - 2 * HEAD_PIPE < len(items):
            stage_finish(items[i - 2 * HEAD_PIPE])


def _mlstm(proj3, gates3, bif, mh_w, tblk):
    b, s, _ = proj3.shape
    n_chunks = tblk // ML_CHUNK

    def col(block):
        return pl.BlockSpec((1, tblk, D_MODEL), lambda bi, ti: (bi, ti, block))

    def full(shape):
        return pl.BlockSpec(shape, lambda bi, ti: (0,) * len(shape))

    ltri = jnp.asarray(np.tril(np.ones((ML_CHUNK, ML_CHUNK), np.float32)), BF16)
    esel = jnp.asarray(_bcast_select_np(), BF16)
    rsel = jnp.asarray(_row_select_np(), BF16)
    return pl.pallas_call(
        functools.partial(_mlstm_kernel, tblk=tblk),
        grid=(b, s // tblk),
        in_specs=[
            col(COL_ML_Q), col(COL_ML_K), col(COL_ML_V),
            pl.BlockSpec((1, tblk, 2 * HEAD_W), lambda bi, ti: (bi, ti, 0)),
            full((2, HEAD_W)), full((1, D_MODEL)),
            full((ML_CHUNK, ML_CHUNK)), full(esel.shape), full(rsel.shape),
        ],
        out_specs=pl.BlockSpec((1, tblk, D_MODEL), lambda bi, ti: (bi, ti, 0)),
        out_shape=jax.ShapeDtypeStruct((b, s, D_MODEL), BF16),
        scratch_shapes=[
            pltpu.VMEM((tblk, N_HEADS * _BCAST_W), F32),
            pltpu.VMEM((n_chunks, HEAD_W, ML_CHUNK), F32),
            pltpu.VMEM((n_chunks, HEAD_W, ML_CHUNK), F32),
            pltpu.VMEM((n_chunks, 8, D_MODEL), F32),
            pltpu.VMEM((N_HEADS, HEAD_W, 2 * HEAD_W), F32),
            pltpu.VMEM((1, HEAD_W), F32),
        ],
        compiler_params=pltpu.CompilerParams(
            dimension_semantics=("arbitrary", "arbitrary"), vmem_limit_bytes=VMEM_LIMIT),
        name="mlstm",
    )(proj3, proj3, proj3, gates3, bif, mh_w, ltri, esel, rsel)


def _out_kernel(ya_ref, hn_ref, og_ref, zg_ref, ga_ref, gm_ref, x_ref, wpa_ref, wpm_ref, wout_ref, bg_ref,
                nf_ref, o_ref, *, final_norm):
    branches = {}

    def project_branches(r):
        rows = slice(r, r + OUT_ROWS)
        zg = zg_ref[rows, :].astype(F32)
        y_m = ((_sigmoid(og_ref[rows, :].astype(F32)) * hn_ref[rows, :].astype(F32))
               * (zg * _sigmoid(zg))).astype(BF16)
        branches[r] = (_dot(ya_ref[rows, :], wpa_ref[...]), _dot(y_m, wpm_ref[...]))

    def merge_and_project(r):
        rows = slice(r, r + OUT_ROWS)
        a, b = branches.pop(r)
        g_a = _sigmoid(ga_ref[rows, :].astype(F32) + bg_ref[0:1, :])
        g_m = _sigmoid(gm_ref[rows, :].astype(F32) + bg_ref[1:2, :])
        hres = x_ref[rows, :] + _dot((g_a * a + g_m * b).astype(BF16), wout_ref[...])
        if final_norm:
            ms = jnp.mean(hres * hres, axis=-1, keepdims=True)
            hres = hres * lax.rsqrt(ms + NORM_EPS) * nf_ref[...]
        o_ref[rows, :] = hres

    tm = o_ref.shape[0]
    project_branches(0)
    for r in range(0, tm, OUT_ROWS):
        if r + OUT_ROWS < tm:
            project_branches(r + OUT_ROWS)
        merge_and_project(r)


def _out_stage(y_a, h_n, proj, x2, w_pa, w_pm, w_out, b_gate, norm_final, tm, final_norm):
    m = x2.shape[0]

    def rows(block=0):
        return pl.BlockSpec((tm, D_MODEL), lambda i: (i, block))

    def full(shape):
        return pl.BlockSpec(shape, lambda i: (0,) * len(shape))

    return pl.pallas_call(
        functools.partial(_out_kernel, final_norm=final_norm),
        grid=(m // tm,),
        in_specs=[
            rows(), rows(), rows(COL_ML_O), rows(COL_ML_Z), rows(COL_GATE_A), rows(COL_GATE_M), rows(),
            full((D_MODEL, D_MODEL)), full((D_MODEL, D_MODEL)), full((D_MODEL, D_MODEL)),
            full((2, D_MODEL)), full((1, D_MODEL)),
        ],
        out_specs=rows(),
        out_shape=jax.ShapeDtypeStruct((m, D_MODEL), F32),
        compiler_params=pltpu.CompilerParams(
            dimension_semantics=("arbitrary",), vmem_limit_bytes=VMEM_LIMIT),
        name="out_stage",
    )(y_a, h_n, proj, proj, proj, proj, x2, w_pa, w_pm, w_out, b_gate, norm_final)


def _transposed_in_weight(w_in, layer):
    w_t = jnp.swapaxes(w_in, 1, 2)
    n_wide = N_MAIN_BLOCKS * D_MODEL
    tail_t = w_t[layer, n_wide + 2 * N_HEADS:, :].astype(BF16)
    reps = HEAD_W // N_HEADS
    gate_t = jnp.concatenate([jnp.tile(w_t[layer, n_wide:n_wide + N_HEADS, :], (reps, 1)),
                              jnp.tile(w_t[layer, n_wide + N_HEADS:n_wide + 2 * N_HEADS, :], (reps, 1))],
                             axis=0).astype(BF16)
    return w_t, tail_t, gate_t


def kernel(x, norm_w, w_in, lam, subln_w, rel_bias, conv_w, conv_b, b_if, mh_w, b_gate, w_pa, w_pm, w_out,
           norm_final):
    b, s, d = x.shape
    depth = norm_w.shape[0]
    m = b * s
    tm_out = min(1024, m)
    tblk = min(1024, s)
    bias_tiles = _bias_tiles(rel_bias)
    h2 = x.reshape(m, d)
    for layer in range(depth):
        lambda_init = 0.8 - 0.6 * math.exp(-0.3 * layer)
        w_t, tail_t, gate_t = _transposed_in_weight(w_in, layer)
        proj, gates = _inproj(h2, norm_w[layer][None, :], w_t, layer, tail_t, gate_t,
                              conv_w[layer], conv_b[layer], s)
        proj3 = proj.reshape(b, s, N_COL_BLOCKS * D_MODEL)
        y_a = _attention(proj3, bias_tiles, lam[layer], subln_w[layer][None, :], lambda_init)
        bif = jnp.tile(b_if[layer], (1, HEAD_W // N_HEADS))
        h_n = _mlstm(proj3, gates.reshape(b, s, 2 * HEAD_W), bif, mh_w[layer][None, :], tblk)
        h2 = _out_stage(y_a.reshape(m, d), h_n.reshape(m, d), proj, h2,
                        w_pa[layer].astype(BF16), w_pm[layer].astype(BF16), w_out[layer].astype(BF16),
                        b_gate[layer], norm_final[None, :], tm_out, final_norm=(layer == depth - 1))
    return h2.reshape(b, s, d)
```

```python
import functools
import math

import numpy as np
import jax
import jax.numpy as jnp
from jax import lax
from jax.experimental import pallas as pl
from jax.experimental.pallas import tpu as pltpu

F32 = jnp.float32
BF16 = jnp.bfloat16

D_MODEL = 1024
N_HEADS = 8
HEAD_W = 128
DA_HEAD_DIM = 64
CHUNK = 64
CONV_K = 4
N_BUCKETS = 32
MAX_DISTANCE = 128
NORM_EPS = 1e-6
SUBLN_EPS = 1e-5
HEAD_LN_EPS = 1e-5
NEG_INF = -1e30
LOG2E = 1.4426950408889634

COL_DA_Q, COL_DA_K, COL_DA_V, COL_DA_Z = 0, 1, 2, 3
COL_ML_Q, COL_ML_K, COL_ML_V, COL_ML_O, COL_ML_Z = 4, 5, 6, 7, 8
COL_GATE_A, COL_GATE_M = 9, 10
N_COL_BLOCKS = 11
N_MAIN_BLOCKS = 9

ATT_BLOCK = 256
ONES_ROWS = 16
ATT_PIPE = 2
ATT_HEADS = 2
ML_CHUNK = 128
HEAD_PIPE = 1
OUT_ROWS = 256
OUT_COLS = 512
VMEM_LIMIT = 56 * 1024 * 1024


def _sigmoid(x):
    return 0.5 * jnp.tanh(0.5 * x) + 0.5


def _dot(a, b):
    return jnp.dot(a, b, preferred_element_type=F32)


def _dot_nt(a, b):
    return lax.dot_general(a, b, (((1,), (1,)), ((), ())), preferred_element_type=F32)


CONV_ROWS = 512
CONV_PAD = 8


def _inproj_kernel(x_ref, nw_ref, w_ref, wt_ref, wg_ref, cw_ref, cb_ref, o_ref, g_ref, xn_ref, y_ref, wb_ref,
                   *, seq):
    j = pl.program_id(1)

    @pl.when(j == 0)
    def _():
        x = x_ref[...]
        ms = jnp.mean(x * x, axis=-1, keepdims=True)
        xn = (x * lax.rsqrt(ms + NORM_EPS) * nw_ref[...]).astype(BF16)
        xn_ref[...] = xn
        g_ref[...] = _dot_nt(xn, wg_ref[...])

    is_conv = (j == COL_ML_Q) | (j == COL_ML_K)

    @pl.when(jnp.logical_not(is_conv) & (j < N_MAIN_BLOCKS))
    def _():
        o_ref[...] = _dot_nt(xn_ref[...], w_ref[...].astype(BF16)).astype(BF16)

    @pl.when(j >= N_MAIN_BLOCKS)
    def _():
        o_ref[...] = _dot_nt(xn_ref[...], wt_ref[...]).astype(BF16)

    @pl.when(is_conv)
    def _():
        which = j - COL_ML_Q
        taps = cw_ref[which]
        bias = cb_ref[pl.ds(which, 1), :]

        y_ref[0:CONV_PAD, :] = jnp.zeros((CONV_PAD, D_MODEL), F32)
        wb_ref[...] = w_ref[...].astype(BF16)
        for r in range(0, seq, CONV_ROWS):
            y_ref[CONV_PAD:, :] = _dot_nt(xn_ref[r:r + CONV_ROWS, :], wb_ref[...])
            win = y_ref[...]
            acc = bias + taps[CONV_K - 1:CONV_K, :] * win[CONV_PAD:, :]
            for back in range(1, CONV_K):
                shifted = pltpu.roll(win, back, 0)[CONV_PAD:, :]
                acc = acc + taps[CONV_K - 1 - back:CONV_K - back, :] * shifted
            o_ref[r:r + CONV_ROWS, :] = (acc * _sigmoid(acc)).astype(BF16)
            y_ref[0:CONV_PAD, :] = win[CONV_ROWS:, :]


def _inproj(x2, norm_w, w_t, layer, tail_t, gate_t, conv_w, conv_b, seq):
    m = x2.shape[0]
    n = N_COL_BLOCKS * D_MODEL
    return pl.pallas_call(
        functools.partial(_inproj_kernel, seq=seq),
        grid=(m // seq, N_COL_BLOCKS),
        in_specs=[
            pl.BlockSpec((seq, D_MODEL), lambda i, j: (i, 0)),
            pl.BlockSpec((1, D_MODEL), lambda i, j: (0, 0)),
            pl.BlockSpec((None, D_MODEL, D_MODEL), lambda i, j: (layer, jnp.minimum(j, N_MAIN_BLOCKS - 1), 0)),
            pl.BlockSpec((D_MODEL, D_MODEL), lambda i, j: (jnp.maximum(j - N_MAIN_BLOCKS, 0), 0)),
            pl.BlockSpec((2 * HEAD_W, D_MODEL), lambda i, j: (0, 0)),
            pl.BlockSpec((2, CONV_K, D_MODEL), lambda i, j: (0, 0, 0)),
            pl.BlockSpec((2, D_MODEL), lambda i, j: (0, 0)),
        ],
        out_specs=[
            pl.BlockSpec((seq, D_MODEL), lambda i, j: (i, j)),
            pl.BlockSpec((seq, 2 * HEAD_W), lambda i, j: (i, 0)),
        ],
        out_shape=[
            jax.ShapeDtypeStruct((m, n), BF16),
            jax.ShapeDtypeStruct((m, 2 * HEAD_W), F32),
        ],
        scratch_shapes=[
            pltpu.VMEM((seq, D_MODEL), BF16),
            pltpu.VMEM((CONV_ROWS + CONV_PAD, D_MODEL), F32),
            pltpu.VMEM((D_MODEL, D_MODEL), BF16),
        ],
        compiler_params=pltpu.CompilerParams(
            dimension_semantics=("arbitrary", "arbitrary"), vmem_limit_bytes=VMEM_LIMIT),
        name="inproj",
    )(x2, norm_w, w_t, tail_t, gate_t, conv_w, conv_b)


def _rel_bucket_np(rel):
    nb = N_BUCKETS // 2
    max_exact = nb // 2
    bucket = np.where(rel > 0, nb, 0)
    n = np.abs(rel)
    nf = np.maximum(n, 1).astype(np.float32)
    large = max_exact + (np.log(nf / np.float32(max_exact)) / np.float32(math.log(MAX_DISTANCE / max_exact))
                         * np.float32(nb - max_exact)).astype(np.int32)
    large = np.minimum(large, nb - 1)
    return bucket + np.where(n < max_exact, n, large)


def _bucket_tiles():
    dk = np.arange(ATT_BLOCK)[:, None]
    dq = np.arange(ATT_BLOCK)[None, :]
    diag = np.where(dk // CHUNK <= dq // CHUNK, _rel_bucket_np(dk - dq), -1)
    prev = _rel_bucket_np(dk - dq - ATT_BLOCK)
    return np.stack([diag, prev]).astype(np.int32)


FAR_BUCKET = int(_rel_bucket_np(np.array([-(ATT_BLOCK + 1)]))[0])


def _bias_kernel(rb_ref, bkt_ref, o_ref):
    h = pl.program_id(0)
    c_far = rb_ref[FAR_BUCKET, h]
    for t in range(2):
        bk = bkt_ref[t]
        acc = jnp.full((ATT_BLOCK, ATT_BLOCK), NEG_INF, F32)
        for n in range(N_BUCKETS):
            acc = jnp.where(bk == n, (rb_ref[n, h] - c_far) * LOG2E, acc)
        o_ref[0, t] = acc


def _bias_tiles(rel_bias):
    return pl.pallas_call(
        _bias_kernel,
        grid=(N_HEADS,),
        in_specs=[
            pl.BlockSpec(memory_space=pltpu.SMEM),
            pl.BlockSpec((2, ATT_BLOCK, ATT_BLOCK), lambda h: (0, 0, 0)),
        ],
        out_specs=pl.BlockSpec((1, 2, ATT_BLOCK, ATT_BLOCK), lambda h: (h, 0, 0, 0)),
        out_shape=jax.ShapeDtypeStruct((N_HEADS, 2, ATT_BLOCK, ATT_BLOCK), F32),
        name="bias_tiles",
    )(rel_bias, jnp.asarray(_bucket_tiles()))


def _attn_kernel(lam_ref, subln_ref, bias_ref, q_ref, k_ref, v_ref, z_ref, o_ref,
                 vt_ref, s_ref, *, seq, lambda_init):
    tb = ATT_BLOCK
    lamv = lam_ref[...]
    lam = (jnp.exp(jnp.sum(lamv[0:1] * lamv[1:2], axis=-1, keepdims=True))
           - jnp.exp(jnp.sum(lamv[2:3] * lamv[3:4], axis=-1, keepdims=True)) + lambda_init)
    head_cols = lambda h: slice(h * HEAD_W, (h + 1) * HEAD_W)

    for h in range(ATT_HEADS):
        for i in range(seq // HEAD_W):
            sl = slice(i * HEAD_W, (i + 1) * HEAD_W)
            vt_ref[h, 0:HEAD_W, sl] = v_ref[0, sl, head_cols(h)].astype(F32).T.astype(BF16)
        vt_ref[h, HEAD_W:, :] = jnp.ones((ONES_ROWS, seq), BF16)

    lane = lax.broadcasted_iota(jnp.int32, (tb, HEAD_W), 1)
    blocks = lambda j: slice(j * tb, (j + 1) * tb)
    units = [(h, qi, p) for qi in range(seq // tb) for h in range(ATT_HEADS) for p in range(2)]
    slot = lambda unit: units.index(unit) % (ATT_PIPE + 1)
    q_halves, col_max, normed = {}, {}, {}

    def logits(h, qi, p):
        if (h, qi) not in q_halves:
            q = (q_ref[0, blocks(qi), head_cols(h)].astype(F32) * (DA_HEAD_DIM ** -0.5 * LOG2E)).astype(BF16)
            q_halves[h, qi] = (jnp.where(lane < DA_HEAD_DIM, q, jnp.zeros_like(q)),
                               jnp.where(lane >= DA_HEAD_DIM, q, jnp.zeros_like(q)))
        s = _dot_nt(k_ref[0, 0:(qi + 1) * tb, head_cols(h)], q_halves[h, qi][p])
        for j in range(qi + 1):
            sj = s[blocks(j)]
            if j >= qi - 1:
                sj = sj + bias_ref[h, qi - j]
            s_ref[slot((h, qi, p)), blocks(j), :] = sj

    def finish_logits(h, qi, p):
        m = None
        for j in range(qi + 1):
            bm = jnp.max(s_ref[slot((h, qi, p)), blocks(j), :].reshape(tb // 8, 8, tb), axis=0)
            m = bm if m is None else jnp.maximum(m, bm)
        col_max[h, qi, p] = jnp.max(m, axis=0, keepdims=True)

    def probs(h, qi, p):
        acc = None
        m = col_max[h, qi, p]
        half = tb // 2
        assert half % CHUNK == 0
        for j in range(qi + 1):
            if j < qi:
                pt = jnp.exp2((s_ref[slot((h, qi, p)), blocks(j), :] - m).astype(BF16))
            else:
                top = jnp.exp2((s_ref[slot((h, qi, p)), j * tb:j * tb + half, :] - m).astype(BF16))
                right = jnp.exp2((s_ref[slot((h, qi, p)), j * tb + half:(j + 1) * tb, half:]
                                  - m[:, half:]).astype(BF16))
                pt = jnp.concatenate(
                    [top, jnp.concatenate([jnp.zeros((half, half), BF16), right], axis=1)], axis=0)
            ba = _dot(vt_ref[h, :, blocks(j)], pt)
            acc = ba if acc is None else acc + ba
        normed[h, qi, p] = acc[0:HEAD_W] * (1.0 / acc[HEAD_W:HEAD_W + 1])

    def epilogue(h, qi):
        o = normed.pop((h, qi, 0)) - lam * normed.pop((h, qi, 1))
        ms = jnp.mean(o * o, axis=0, keepdims=True)
        on = (o * lax.rsqrt(ms + SUBLN_EPS)).T
        z = z_ref[0, blocks(qi), head_cols(h)].astype(F32)
        y = on * subln_ref[...] * (1.0 - lambda_init) * (z * _sigmoid(z))
        o_ref[0, blocks(qi), head_cols(h)] = y.astype(o_ref.dtype)

    for r in range(len(units) + ATT_PIPE):
        unit = units[r] if r < len(units) else None
        late = units[r - ATT_PIPE] if r >= ATT_PIPE else None
        if unit is not None:
            logits(*unit)
        if late is not None:
            probs(*late)
        if unit is not None:
            finish_logits(*unit)
        if late is not None and late[2] == 1:
            epilogue(late[0], late[1])


def _attention(proj3, bias_tiles, lam, subln_w, lambda_init):
    b, s, _ = proj3.shape
    width = ATT_HEADS * HEAD_W

    def col(block):
        return pl.BlockSpec((1, s, width), lambda bi, hi: (bi, 0, block * (N_HEADS // ATT_HEADS) + hi))

    return pl.pallas_call(
        functools.partial(_attn_kernel, seq=s, lambda_init=lambda_init),
        grid=(b, N_HEADS // ATT_HEADS),
        in_specs=[
            pl.BlockSpec((4, DA_HEAD_DIM), lambda bi, hi: (0, 0)),
            pl.BlockSpec((1, HEAD_W), lambda bi, hi: (0, 0)),
            pl.BlockSpec((ATT_HEADS, 2, ATT_BLOCK, ATT_BLOCK), lambda bi, hi: (hi, 0, 0, 0)),
            col(COL_DA_Q), col(COL_DA_K), col(COL_DA_V), col(COL_DA_Z),
        ],
        out_specs=pl.BlockSpec((1, s, width), lambda bi, hi: (bi, 0, hi)),
        out_shape=jax.ShapeDtypeStruct((b, s, D_MODEL), BF16),
        scratch_shapes=[
            pltpu.VMEM((ATT_HEADS, HEAD_W + ONES_ROWS, s), BF16),
            pltpu.VMEM((ATT_PIPE + 1, s, ATT_BLOCK), F32),
        ],
        compiler_params=pltpu.CompilerParams(
            dimension_semantics=("arbitrary", "arbitrary"), vmem_limit_bytes=VMEM_LIMIT),
        name="diff_attention",
    )(lam, subln_w, bias_tiles, proj3, proj3, proj3, proj3)


_BCAST_PARTS = (3, 3, 2)
_DECAY_PARTS = 2


_BCAST_W = len(_BCAST_PARTS) * HEAD_W


def _bcast_select_np():
    e = np.zeros((HEAD_W, N_HEADS * _BCAST_W), np.float32)
    g = 0
    for o, parts in enumerate(_BCAST_PARTS):
        for _ in range(parts):
            for h in range(N_HEADS):
                e[8 * g + h, h * _BCAST_W + o * HEAD_W:h * _BCAST_W + (o + 1) * HEAD_W] = 1.0
            g += 1
    return e


def _row_select_np():
    e = np.zeros((HEAD_W, N_HEADS * HEAD_W), np.float32)
    for g in range(_DECAY_PARTS):
        for h in range(N_HEADS):
            e[8 * g + h, h * HEAD_W:(h + 1) * HEAD_W] = 1.0
    return e


def _split_parts(x, n):
    parts = []
    r = x
    for i in range(n):
        p = r.astype(BF16).astype(F32)
        parts.append(p)
        if i + 1 < n:
            r = r - p
    return parts


def _pack_groups(arrays_and_parts, group):
    parts = []
    for x, n in arrays_and_parts:
        parts.extend(_split_parts(x, n))
    packed = jnp.zeros_like(parts[0])
    for gi, p in enumerate(parts):
        packed = jnp.where(group == gi, p, packed)
    return packed.astype(BF16)


def _mlstm_kernel(q_ref, k_ref, v_ref, g_ref, bif_ref, mhw_ref, ltri_ref, esel_ref, rsel_ref, y_ref,
                  bc_ref, wpt_ref, wst_ref, dcrow_ref, c_ref, m_ref, *, tblk):
    L = ML_CHUNK
    n_chunks = tblk // L

    @pl.when(pl.program_id(1) == 0)
    def _():
        c_ref[...] = jnp.zeros_like(c_ref)
        m_ref[...] = jnp.zeros_like(m_ref)

    row = lax.broadcasted_iota(jnp.int32, (tblk, HEAD_W), 0) % L
    group = lax.broadcasted_iota(jnp.int32, (tblk, HEAD_W), 1) // 8
    ig = g_ref[0, :, 0:HEAD_W] + bif_ref[0:1, :]
    fg = g_ref[0, :, HEAD_W:2 * HEAD_W] + bif_ref[1:2, :]
    lf = jnp.minimum(fg, 0.0) - jnp.log(1.0 + jnp.exp(-jnp.abs(fg)))
    ltri = ltri_ref[...]
    lf_parts = [p.astype(BF16) for p in _split_parts(lf, 3)]
    bcum = jnp.concatenate(
        [sum(_dot(ltri, p[c * L:(c + 1) * L]) for p in lf_parts) for c in range(n_chunks)], axis=0)
    wp = ig - bcum
    cmax = wp
    sh = 1
    while sh < L:
        cmax = jnp.where(row >= sh, jnp.maximum(cmax, pltpu.roll(cmax, sh, 0)), cmax)
        sh *= 2
    m_rows, b_last_rows, m_next_rows = [], [], []
    m_state = m_ref[...]
    for c in range(n_chunks):
        last = (c + 1) * L - 1
        b_last = bcum[last:last + 1, :]
        m_next = b_last + jnp.maximum(m_state, cmax[last:last + 1, :])
        dcrow_ref[c] = _dot(
            _pack_groups([(jnp.broadcast_to(jnp.exp(b_last + m_state - m_next), (8, HEAD_W)), _DECAY_PARTS)],
                         group[0:8]), rsel_ref[...])
        m_rows.append(jnp.broadcast_to(m_state, (L, HEAD_W)))
        b_last_rows.append(jnp.broadcast_to(b_last, (L, HEAD_W)))
        m_next_rows.append(jnp.broadcast_to(m_next, (L, HEAD_W)))
        m_state = m_next
    m_ref[...] = m_state
    m_prev = jnp.concatenate(m_rows, axis=0)
    inter = bcum + m_prev
    m_t = jnp.maximum(inter, bcum + cmax)
    w_inter = jnp.exp(inter - m_t)
    w_state = jnp.exp(jnp.concatenate(b_last_rows, axis=0) + wp - jnp.concatenate(m_next_rows, axis=0))
    k_scale = HEAD_W ** -0.5
    bc_ref[...] = _dot(_pack_groups(
        [((bcum - m_t) * LOG2E + math.log2(k_scale), 3), (m_t * -LOG2E, 3), (w_inter * k_scale, 2)], group),
        esel_ref[...])
    wp2 = wp * LOG2E
    for c in range(n_chunks):
        wpt_ref[c] = wp2[c * L:(c + 1) * L].T
        wst_ref[c] = w_state[c * L:(c + 1) * L].T

    causal = (lax.broadcasted_iota(jnp.int32, (L, HEAD_W), 1)
              <= lax.broadcasted_iota(jnp.int32, (L, HEAD_W), 0))
    ones_blk = jnp.ones((L, HEAD_W), BF16)

    items = [dict(c=c, h=h, rows=slice(c * L, (c + 1) * L), hs=slice(h * HEAD_W, (h + 1) * HEAD_W))
             for c in range(n_chunks) for h in range(N_HEADS)]

    def bcast(st, o):
        h = st["h"]
        return bc_ref[st["rows"], h * _BCAST_W + o * HEAD_W:h * _BCAST_W + (o + 1) * HEAD_W]

    def stage_scores(st):
        h = st["h"]
        st["q"], st["k"], st["v"] = (ref[0, st["rows"], st["hs"]] for ref in (q_ref, k_ref, v_ref))
        st["s"] = _dot_nt(st["q"], st["k"])
        k_t = (st["k"].astype(F32).T * wst_ref[st["c"], h:h + 1, :]).astype(BF16)
        st["upd"] = _dot(k_t, jnp.concatenate([st["v"], ones_blk], axis=1))

    def stage_readout(st):
        h = st["h"]
        w_intra = jnp.exp2(jnp.where(causal, bcast(st, 0) + wpt_ref[st["c"], h:h + 1, :], NEG_INF))
        scores = (st["s"] * w_intra).astype(BF16)
        lhs = jnp.concatenate([scores, (st["q"].astype(F32) * bcast(st, 2)).astype(BF16)], axis=1)
        st["c_old"] = c_ref[h]
        rhs = jnp.concatenate(
            [jnp.concatenate([st["v"], ones_blk], axis=1), st["c_old"].astype(BF16)], axis=0)
        st["tot"] = _dot(lhs, rhs)

    def stage_finish(st):
        tot = st["tot"]
        hh = tot[:, 0:HEAD_W] / jnp.maximum(jnp.abs(tot[:, HEAD_W:]), jnp.exp2(bcast(st, 1)))
        mu = jnp.mean(hh, axis=-1, keepdims=True)
        hc = hh - mu
        var = jnp.mean(hc * hc, axis=-1, keepdims=True)
        y_ref[0, st["rows"], st["hs"]] = (
            hc * lax.rsqrt(var + HEAD_LN_EPS) * mhw_ref[:, st["hs"]]).astype(y_ref.dtype)
        dc = dcrow_ref[st["c"], 0:1, st["hs"]]
        c_ref[st["h"]] = jnp.concatenate([dc, dc], axis=1) * st["c_old"] + st["upd"]
        st.clear()

    assert 2 * HEAD_PIPE < N_HEADS
    for i in range(len(items) + 2 * HEAD_PIPE):
        if i < len(items):
            stage_scores(items[i])
        if 0 <= i - HEAD_PIPE < len(items):
            stage_readout(items[i - HEAD_PIPE])
        if 0 <= i - 2 * HEAD_PIPE < len(items):
            stage_finish(items[i - 2 * HEAD_PIPE])


def _mlstm(proj3, gates3, bif, mh_w, tblk):
    b, s, _ = proj3.shape
    n_chunks = tblk // ML_CHUNK

    def col(block):
        return pl.BlockSpec((1, tblk, D_MODEL), lambda bi, ti: (bi, ti, block))

    def full(shape):
        return pl.BlockSpec(shape, lambda bi, ti: (0,) * len(shape))

    ltri = jnp.asarray(np.tril(np.ones((ML_CHUNK, ML_CHUNK), np.float32)), BF16)
    esel = jnp.asarray(_bcast_select_np(), BF16)
    rsel = jnp.asarray(_row_select_np(), BF16)
    return pl.pallas_call(
        functools.partial(_mlstm_kernel, tblk=tblk),
        grid=(b, s // tblk),
        in_specs=[
            col(COL_ML_Q), col(COL_ML_K), col(COL_ML_V),
            pl.BlockSpec((1, tblk, 2 * HEAD_W), lambda bi, ti: (bi, ti, 0)),
            full((2, HEAD_W)), full((1, D_MODEL)),
            full((ML_CHUNK, ML_CHUNK)), full(esel.shape), full(rsel.shape),
        ],
        out_specs=pl.BlockSpec((1, tblk, D_MODEL), lambda bi, ti: (bi, ti, 0)),
        out_shape=jax.ShapeDtypeStruct((b, s, D_MODEL), BF16),
        scratch_shapes=[
            pltpu.VMEM((tblk, N_HEADS * _BCAST_W), F32),
            pltpu.VMEM((n_chunks, HEAD_W, ML_CHUNK), F32),
            pltpu.VMEM((n_chunks, HEAD_W, ML_CHUNK), F32),
            pltpu.VMEM((n_chunks, 8, D_MODEL), F32),
            pltpu.VMEM((N_HEADS, HEAD_W, 2 * HEAD_W), F32),
            pltpu.VMEM((1, HEAD_W), F32),
        ],
        compiler_params=pltpu.CompilerParams(
            dimension_semantics=("arbitrary", "arbitrary"), vmem_limit_bytes=VMEM_LIMIT),
        name="mlstm",
    )(proj3, proj3, proj3, gates3, bif, mh_w, ltri, esel, rsel)


def _out_kernel(ya_ref, hn_ref, og_ref, zg_ref, ga_ref, gm_ref, x_ref, wpa_ref, wpm_ref, wout_ref, bg_ref,
                nf_ref, o_ref, *, final_norm):
    branches = {}

    def project_branches(r):
        rows = slice(r, r + OUT_ROWS)
        zg = zg_ref[rows, :].astype(F32)
        y_m = ((_sigmoid(og_ref[rows, :].astype(F32)) * hn_ref[rows, :].astype(F32))
               * (zg * _sigmoid(zg))).astype(BF16)
        merged = []
        for c0 in range(0, D_MODEL, OUT_COLS):
            cols = slice(c0, c0 + OUT_COLS)
            g_a = _sigmoid(ga_ref[rows, cols].astype(F32) + bg_ref[0:1, cols])
            g_m = _sigmoid(gm_ref[rows, cols].astype(F32) + bg_ref[1:2, cols])
            merged.append((g_a * _dot(ya_ref[rows, :], wpa_ref[:, cols])
                           + g_m * _dot(y_m, wpm_ref[:, cols])).astype(BF16))
        branches[r] = jnp.concatenate(merged, axis=1)

    def merge_and_project(r):
        rows = slice(r, r + OUT_ROWS)
        hres = x_ref[rows, :] + _dot(branches.pop(r), wout_ref[...])
        if final_norm:
            ms = jnp.mean(hres * hres, axis=-1, keepdims=True)
            hres = hres * lax.rsqrt(ms + NORM_EPS) * nf_ref[...]
        o_ref[rows, :] = hres

    tm = o_ref.shape[0]
    project_branches(0)
    for r in range(0, tm, OUT_ROWS):
        if r + OUT_ROWS < tm:
            project_branches(r + OUT_ROWS)
        merge_and_project(r)


def _out_stage(y_a, h_n, proj, x2, w_pa, w_pm, w_out, b_gate, norm_final, tm, final_norm):
    m = x2.shape[0]

    def rows(block=0):
        return pl.BlockSpec((tm, D_MODEL), lambda i: (i, block))

    def full(shape):
        return pl.BlockSpec(shape, lambda i: (0,) * len(shape))

    return pl.pallas_call(
        functools.partial(_out_kernel, final_norm=final_norm),
        grid=(m // tm,),
        in_specs=[
            rows(), rows(), rows(COL_ML_O), rows(COL_ML_Z), rows(COL_GATE_A), rows(COL_GATE_M), rows(),
            full((D_MODEL, D_MODEL)), full((D_MODEL, D_MODEL)), full((D_MODEL, D_MODEL)),
            full((2, D_MODEL)), full((1, D_MODEL)),
        ],
        out_specs=rows(),
        out_shape=jax.ShapeDtypeStruct((m, D_MODEL), F32),
        compiler_params=pltpu.CompilerParams(
            dimension_semantics=("arbitrary",), vmem_limit_bytes=VMEM_LIMIT),
        name="out_stage",
    )(y_a, h_n, proj, proj, proj, proj, x2, w_pa, w_pm, w_out, b_gate, norm_final)


def _transposed_in_weight(w_in, layer):
    w_t = jnp.swapaxes(w_in, 1, 2)
    n_wide = N_MAIN_BLOCKS * D_MODEL
    tail_t = w_t[layer, n_wide + 2 * N_HEADS:, :].astype(BF16)
    reps = HEAD_W // N_HEADS
    gate_t = jnp.concatenate([jnp.tile(w_t[layer, n_wide:n_wide + N_HEADS, :], (reps, 1)),
                              jnp.tile(w_t[layer, n_wide + N_HEADS:n_wide + 2 * N_HEADS, :], (reps, 1))],
                             axis=0).astype(BF16)
    return w_t, tail_t, gate_t


def kernel(x, norm_w, w_in, lam, subln_w, rel_bias, conv_w, conv_b, b_if, mh_w, b_gate, w_pa, w_pm, w_out,
           norm_final):
    b, s, d = x.shape
    depth = norm_w.shape[0]
    m = b * s
    tm_out = min(1024, m)
    tblk = min(1024, s)
    bias_tiles = _bias_tiles(rel_bias)
    h2 = x.reshape(m, d)
    for layer in range(depth):
        lambda_init = 0.8 - 0.6 * math.exp(-0.3 * layer)
        w_t, tail_t, gate_t = _transposed_in_weight(w_in, layer)
        proj, gates = _inproj(h2, norm_w[layer][None, :], w_t, layer, tail_t, gate_t,
                              conv_w[layer], conv_b[layer], s)
        proj3 = proj.reshape(b, s, N_COL_BLOCKS * D_MODEL)
        y_a = _attention(proj3, bias_tiles, lam[layer], subln_w[layer][None, :], lambda_init)
        bif = jnp.tile(b_if[layer], (1, HEAD_W // N_HEADS))
        h_n = _mlstm(proj3, gates.reshape(b, s, 2 * HEAD_W), bif, mh_w[layer][None, :], tblk)
        h2 = _out_stage(y_a.reshape(m, d), h_n.reshape(m, d), proj, h2,
                        w_pa[layer].astype(BF16), w_pm[layer].astype(BF16), w_out[layer].astype(BF16),
                        b_gate[layer], norm_final[None, :], tm_out, final_norm=(layer == depth - 1))
    return h2.reshape(b, s, d)
```
